```python
import functools
import jax, jax.numpy as jnp
from jax import lax
import numpy as np

D_MODEL = 1024
BATCH = 2
SEQ = 8192
DEPTH = 4
DEC_BATCH = 128
DEC_SEQ = 1
PAST_LEN = 2048
PAGE_SIZE = 128

W_A = 512
N_HEADS = 8
HEAD_DIM = 64
W_B = N_HEADS * HEAD_DIM
W_C = 512
N_BLOCKS_C = 8
BLOCK_C = W_C // N_BLOCKS_C
CONV_A = 3
CONV_C = 4
CONV_F = 3
D_FF = 3 * D_MODEL
LRU_C = 8.0
Q_BLOCK = 128
FORGET_BIAS = 3.0
EPS = 1e-6
IN_SIZES = (W_A, W_A, W_A, W_B, W_B, W_B, N_HEADS, W_C, W_C, 3 * D_MODEL)
N_IN = sum(IN_SIZES)

kernel_name = "hybrid_conv_fox_rglru_decode_step"


def rmsnorm(x, g):
    xf = x.astype(jnp.float32)
    y = xf * lax.rsqrt(jnp.mean(xf * xf, axis=-1, keepdims=True) + EPS)
    return (y * g.astype(jnp.float32)).astype(x.dtype)


def causal_dwconv(u, hist, w, b=None):
    K = w.shape[0]
    T = u.shape[1]
    ext = jnp.concatenate([hist.astype(u.dtype), u], axis=1)
    y = sum(ext[:, k:k + T] * w[k] for k in range(K))
    if b is not None:
        y = y + b
    return y, ext[:, ext.shape[1] - (K - 1):]


def rglru(xc, h0, w_r, b_r, w_i, b_i, lam):
    B, T, _ = xc.shape
    xb = xc.reshape(B, T, N_BLOCKS_C, BLOCK_C)
    r = jax.nn.sigmoid((jnp.einsum('btnd,nde->btne', xb, w_r).reshape(B, T, W_C) + b_r).astype(jnp.float32))
    i = jax.nn.sigmoid((jnp.einsum('btnd,nde->btne', xb, w_i).reshape(B, T, W_C) + b_i).astype(jnp.float32))
    log_a = -LRU_C * r * jax.nn.softplus(-lam.astype(jnp.float32))
    a = jnp.exp(log_a)
    bterm = jnp.sqrt(-jnp.expm1(2.0 * log_a)) * i * xc.astype(jnp.float32)
    bterm = bterm.at[:, 0].add(a[:, 0] * h0.astype(jnp.float32))

    def combine(lft, rgt):
        return (lft[0] * rgt[0], rgt[0] * lft[1] + rgt[1])

    _, h = lax.associative_scan(combine, (a, bterm), axis=1)
    return h.astype(xc.dtype), h[:, -1].astype(xc.dtype)


def fox_attend(q, k, v, cq, ck, q_pos, k_pos):
    s = jnp.einsum('bqhd,bkhd->bhqk', q, k, preferred_element_type=jnp.float32) * (HEAD_DIM ** -0.5)
    s = s + (jnp.swapaxes(cq, 1, 2)[:, :, :, None] - jnp.swapaxes(ck, 1, 2)[:, :, None, :])
    mask = k_pos[None, :] <= q_pos[:, None]
    s = jnp.where(mask[None, None], s, -jnp.inf)
    p = jax.nn.softmax(s, axis=-1)
    return jnp.einsum('bhqk,bkhd->bqhd', p.astype(v.dtype), v)


def fox_prompt(q, k, v, logf):
    B, T = q.shape[:2]
    c = jnp.cumsum(logf, axis=1)
    k_pos = jnp.arange(T)

    def block(i):
        s0 = i * Q_BLOCK
        qb = lax.dynamic_slice_in_dim(q, s0, Q_BLOCK, axis=1)
        cb = lax.dynamic_slice_in_dim(c, s0, Q_BLOCK, axis=1)
        return fox_attend(qb, k, v, cb, c, s0 + jnp.arange(Q_BLOCK), k_pos)

    o = lax.map(block, jnp.arange(T // Q_BLOCK))
    return jnp.moveaxis(o, 0, 1).reshape(B, T, N_HEADS, HEAD_DIM)


def fox_sample(q, k_new, v_new, logf_new, cache_k=None, cache_v=None, cache_logf=None, page_table=None):
    Bd, T = q.shape[:2]
    past = page_table.shape[1] * PAGE_SIZE
    kp = cache_k[page_table].reshape(Bd, past, N_HEADS, HEAD_DIM)
    vp = cache_v[page_table].reshape(Bd, past, N_HEADS, HEAD_DIM)
    lfp = cache_logf[page_table].reshape(Bd, past, N_HEADS).astype(jnp.float32)
    k = jnp.concatenate([kp.astype(k_new.dtype), k_new], axis=1)
    v = jnp.concatenate([vp.astype(v_new.dtype), v_new], axis=1)
    c = jnp.cumsum(jnp.concatenate([lfp, logf_new], axis=1), axis=1)
    return fox_attend(q, k, v, c[:, past:], c, past + jnp.arange(T), jnp.arange(past + T))


def trunk_layer(x, p, attend, conv_a_hist, rg_conv_hist, rg_h0, ffn_hist):
    B, T, _ = x.shape
    h = rmsnorm(x, p['norm1_g'])
    proj = h @ p['w_in']
    offs = []
    acc = 0
    for s in IN_SIZES[:-1]:
        acc += s
        offs.append(acc)
    bA, cA, xA, q, k, v, f, xC, gC, gates = jnp.split(proj, offs, axis=-1)
    ua, new_conv_a = causal_dwconv(cA * xA, conv_a_hist, p['conv_a_w'])
    yA = bA * ua
    q = q.reshape(B, T, N_HEADS, HEAD_DIM)
    k = k.reshape(B, T, N_HEADS, HEAD_DIM)
    v = v.reshape(B, T, N_HEADS, HEAD_DIM)
    logf = jax.nn.log_sigmoid((f + p['b_forget']).astype(jnp.float32))
    yB = attend(q, k, v, logf).reshape(B, T, W_B)
    xc, new_rg_conv = causal_dwconv(xC, rg_conv_hist, p['rg_conv_w'], p['rg_conv_b'])
    hc, h_last = rglru(xc, rg_h0, p['rg_w_r'], p['rg_b_r'], p['rg_w_i'], p['rg_b_i'], p['rg_lambda'])
    yC = hc * jax.nn.gelu(gC)
    g = jax.nn.sigmoid(gates.astype(jnp.float32)).astype(x.dtype).reshape(B, T, 3, D_MODEL)
    wb = p['w_branch']
    merged = (g[:, :, 0] * (yA @ wb[:W_A])
              + g[:, :, 1] * (yB @ wb[W_A:W_A + W_B])
              + g[:, :, 2] * (yC @ wb[W_A + W_B:]))
    x = x + merged @ p['w_out']
    h2 = rmsnorm(x, p['norm2_g'])
    up, new_ffn = causal_dwconv(h2 @ p['ffn_w_up'], ffn_hist, p['ffn_conv_w'], p['ffn_conv_b'])
    a_half, b_half = jnp.split(up, 2, axis=-1)
    x = x + (jax.nn.gelu(a_half) * b_half) @ p['ffn_w_down']
    return x, (new_conv_a, k, v, logf.astype(x.dtype), new_rg_conv, h_last, new_ffn)


def setup_inputs(seed: int = 0) -> dict:
    key = jax.random.key(seed)
    ks = jax.random.split(key, 40)

    def nrm(i, shape, scale):
        return scale * jax.random.normal(ks[i], shape, jnp.float32)

    n_pages = PAST_LEN // PAGE_SIZE
    n_used = DEC_BATCH * n_pages
    n_phys = (n_used * 5) // 4
    perm = jax.random.permutation(ks[0], n_phys)
    page_table = perm[:n_used].reshape(DEC_BATCH, n_pages).astype(jnp.int32)
    a0 = jax.random.uniform(ks[1], (DEPTH, W_C), jnp.float32, minval=0.9, maxval=0.999)
    return {
        "x_prompt": nrm(2, (BATCH, SEQ, D_MODEL), 1.0),
        "x_sample": nrm(3, (DEC_BATCH, DEC_SEQ, D_MODEL), 1.0),
        "cache_k": nrm(4, (DEPTH, n_phys, PAGE_SIZE, N_HEADS, HEAD_DIM), 1.0),
        "cache_v": nrm(5, (DEPTH, n_phys, PAGE_SIZE, N_HEADS, HEAD_DIM), 1.0),
        "cache_logf": jax.nn.log_sigmoid(FORGET_BIAS + nrm(6, (DEPTH, n_phys, PAGE_SIZE, N_HEADS), 1.0)),
        "page_table": page_table,
        "state_conv_a": nrm(7, (DEPTH, DEC_BATCH, CONV_A - 1, W_A), 1.0),
        "state_rg_conv": nrm(8, (DEPTH, DEC_BATCH, CONV_C - 1, W_C), 1.0),
        "state_rg_h": nrm(9, (DEPTH, DEC_BATCH, W_C), 0.5),
        "state_ffn_conv": nrm(10, (DEPTH, DEC_BATCH, CONV_F - 1, 2 * D_FF), 1.0),
        "norm1_g": 1.0 + nrm(11, (DEPTH, D_MODEL), 0.01),
        "w_in": nrm(12, (DEPTH, D_MODEL, N_IN), D_MODEL ** -0.5),
        "conv_a_w": nrm(13, (DEPTH, CONV_A, W_A), CONV_A ** -0.5),
        "b_forget": FORGET_BIAS + nrm(14, (DEPTH, N_HEADS), 0.1),
        "rg_conv_w": nrm(15, (DEPTH, CONV_C, W_C), CONV_C ** -0.5),
        "rg_conv_b": nrm(16, (DEPTH, W_C), 0.01),
        "rg_w_r": nrm(17, (DEPTH, N_BLOCKS_C, BLOCK_C, BLOCK_C), BLOCK_C ** -0.5),
        "rg_b_r": nrm(18, (DEPTH, W_C), 0.01),
        "rg_w_i": nrm(19, (DEPTH, N_BLOCKS_C, BLOCK_C, BLOCK_C), BLOCK_C ** -0.5),
        "rg_b_i": nrm(20, (DEPTH, W_C), 0.01),
        "rg_lambda": jnp.log(a0) - jnp.log1p(-a0),
        "w_branch": nrm(21, (DEPTH, W_A + W_B + W_C, D_MODEL), W_A ** -0.5),
        "w_out": nrm(22, (DEPTH, D_MODEL, D_MODEL), D_MODEL ** -0.5),
        "norm2_g": 1.0 + nrm(23, (DEPTH, D_MODEL), 0.01),
        "ffn_w_up": nrm(24, (DEPTH, D_MODEL, 2 * D_FF), D_MODEL ** -0.5),
        "ffn_conv_w": nrm(25, (DEPTH, CONV_F, 2 * D_FF), CONV_F ** -0.5),
        "ffn_conv_b": nrm(26, (DEPTH, 2 * D_FF), 0.01),
        "ffn_w_down": nrm(27, (DEPTH, D_FF, D_MODEL), D_FF ** -0.5),
        "final_norm_g": 1.0 + nrm(28, (D_MODEL,), 0.01),
    }


def reference(x_prompt, x_sample, cache_k, cache_v, cache_logf, page_table,
              state_conv_a, state_rg_conv, state_rg_h, state_ffn_conv,
              norm1_g, w_in, conv_a_w, b_forget, rg_conv_w, rg_conv_b,
              rg_w_r, rg_b_r, rg_w_i, rg_b_i, rg_lambda, w_branch, w_out,
              norm2_g, ffn_w_up, ffn_conv_w, ffn_conv_b, ffn_w_down, final_norm_g):
    xp, xs = x_prompt, x_sample
    Bp = xp.shape[0]
    dt = xp.dtype
    sp, ss = [], []
    for l in range(DEPTH):
        p = {
            'norm1_g': norm1_g[l], 'w_in': w_in[l], 'conv_a_w': conv_a_w[l], 'b_forget': b_forget[l],
            'rg_conv_w': rg_conv_w[l], 'rg_conv_b': rg_conv_b[l], 'rg_w_r': rg_w_r[l], 'rg_b_r': rg_b_r[l],
            'rg_w_i': rg_w_i[l], 'rg_b_i': rg_b_i[l], 'rg_lambda': rg_lambda[l], 'w_branch': w_branch[l],
            'w_out': w_out[l], 'norm2_g': norm2_g[l], 'ffn_w_up': ffn_w_up[l], 'ffn_conv_w': ffn_conv_w[l],
            'ffn_conv_b': ffn_conv_b[l], 'ffn_w_down': ffn_w_down[l],
        }
        xp, st_p = trunk_layer(
            xp, p, fox_prompt,
            jnp.zeros((Bp, CONV_A - 1, W_A), dt), jnp.zeros((Bp, CONV_C - 1, W_C), dt),
            jnp.zeros((Bp, W_C), dt), jnp.zeros((Bp, CONV_F - 1, 2 * D_FF), dt))
        attend_s = functools.partial(fox_sample, cache_k=cache_k[l], cache_v=cache_v[l],
                                     cache_logf=cache_logf[l], page_table=page_table)
        xs, st_s = trunk_layer(xs, p, attend_s, state_conv_a[l], state_rg_conv[l],
                               state_rg_h[l], state_ffn_conv[l])
        sp.append(st_p)
        ss.append(st_s)
    y_prompt = rmsnorm(xp, final_norm_g)
    y_sample = rmsnorm(xs, final_norm_g)

    def stk(outs, i):
        return jnp.stack([o[i] for o in outs], axis=0)

    return (y_prompt, y_sample,
            stk(sp, 0), stk(sp, 1), stk(sp, 2), stk(sp, 3), stk(sp, 4), stk(sp, 5), stk(sp, 6),
            stk(ss, 0), stk(ss, 1), stk(ss, 2), stk(ss, 3), stk(ss, 4), stk(ss, 5), stk(ss, 6))
```

```python
import functools

import jax
import jax.numpy as jnp
from jax import lax
from jax.experimental import pallas as pl
from jax.experimental.pallas import tpu as pltpu

F32 = jnp.float32
BF16 = jnp.bfloat16

EPS = 1e-6
LRU_C = 8.0
N_HEADS = 8
HEAD_DIM = 64
W_MIX = N_HEADS * HEAD_DIM
NEG = -1e30
LANES = 128
SUBLANES = 8
VMEM_LIMIT = 56 * 1024 * 1024

COL_BA, COL_CA, COL_XA, COL_Q, COL_K, COL_V, COL_XC, COL_GC = range(8)
COL_GATES = 4


def _params(*sem):
    return pltpu.CompilerParams(dimension_semantics=sem, vmem_limit_bytes=VMEM_LIMIT)


def _rmsnorm(x, g):
    ms = jnp.mean(x * x, axis=-1, keepdims=True)
    return x * lax.rsqrt(ms + EPS) * g


def _softplus(y):
    return jnp.maximum(y, 0.0) + jnp.log1p(jnp.exp(-jnp.abs(y)))


def _log_sigmoid(z):
    return -_softplus(-z)


def _bdot(a, b):
    return jnp.dot(a.astype(BF16), b.astype(BF16), preferred_element_type=F32)


def _shift_rows(x, k, tail):
    y = pltpu.roll(x, k, axis=0)
    row = lax.broadcasted_iota(jnp.int32, tail.shape, 0)
    first = jnp.where(row < k, pltpu.roll(tail, k, axis=0), y[:SUBLANES])
    return jnp.concatenate([first, y[SUBLANES:]], axis=0)


def _inproj_kernel(*refs, with_vt):
    if with_vt:
        x_ref, g_ref, w_ref, wf_ref, wvt_ref, proj_ref, f_ref, vt_ref, h_scr = refs
    else:
        x_ref, g_ref, w_ref, wf_ref, proj_ref, f_ref, h_scr = refs

    @pl.when(pl.program_id(1) == 0)
    def _():
        hb = _rmsnorm(x_ref[...], g_ref[...]).astype(BF16)
        h_scr[...] = hb
        f_ref[...] = jnp.dot(hb, wf_ref[...], preferred_element_type=F32)
        if with_vt:
            vt = lax.dot_general(wvt_ref[...], hb, (((1,), (1,)), ((), ())), preferred_element_type=F32)
            vt_ref[...] = vt.astype(BF16)

    proj_ref[...] = jnp.dot(h_scr[...], w_ref[...], preferred_element_type=F32)


def _inproj(x, g, w, wf, tm, tn, wvt=None):
    n, d = x.shape
    nw = w.shape[1]
    in_specs = [
        pl.BlockSpec((tm, d), lambda i, j: (i, 0)),
        pl.BlockSpec((1, d), lambda i, j: (0, 0)),
        pl.BlockSpec((d, tn), lambda i, j: (0, j)),
        pl.BlockSpec((d, LANES), lambda i, j: (0, 0)),
    ]
    out_shape = [jax.ShapeDtypeStruct((n, nw), F32), jax.ShapeDtypeStruct((n, LANES), F32)]
    out_specs = [pl.BlockSpec((tm, tn), lambda i, j: (i, j)), pl.BlockSpec((tm, LANES), lambda i, j: (i, 0))]
    args = [x, g, w, wf]
    if wvt is not None:
        in_specs.append(pl.BlockSpec((W_MIX, d), lambda i, j: (0, 0)))
        out_shape.append(jax.ShapeDtypeStruct((W_MIX, n), BF16))
        out_specs.append(pl.BlockSpec((W_MIX, tm), lambda i, j: (0, i)))
        args.append(wvt)
    return pl.pallas_call(
        functools.partial(_inproj_kernel, with_vt=wvt is not None),
        out_shape=tuple(out_shape),
        grid=(n // tm, nw // tn),
        in_specs=in_specs,
        out_specs=tuple(out_specs),
        scratch_shapes=[pltpu.VMEM((tm, d), BF16)],
        compiler_params=_params("parallel", "arbitrary"),
        name="inproj",
    )(*args)


def _attn_prep_kernel(f_ref, b_ref, q_ref, k_ref, place_ref, ones_ref, e_ref, logf_ref, qa_ref, ka_ref, carry):
    @pl.when(pl.program_id(1) == 0)
    def _():
        carry[...] = jnp.zeros_like(carry)

    lf = _log_sigmoid(f_ref[...] + b_ref[...])
    logf_ref[...] = lf
    tc = lf.shape[0]
    row = lax.broadcasted_iota(jnp.int32, lf.shape, 0)
    c = lf
    s = 1
    while s < tc:
        c = c + jnp.where(row >= s, pltpu.roll(c, s, axis=0), 0.0)
        s *= 2
    c = c + carry[...]
    carry[...] = c[tc - 1:tc, :]
    c_hi = c.astype(BF16)
    r1 = c - c_hi.astype(F32)
    c_mid = r1.astype(BF16)
    c_lo = (r1 - c_mid.astype(F32)).astype(BF16)
    qb = (q_ref[...] * (HEAD_DIM ** -0.5)).astype(BF16)
    qa_ref[...] = (jnp.dot(qb, place_ref[...], preferred_element_type=F32) + ones_ref[...]).astype(BF16)
    ka = (jnp.dot(k_ref[...].astype(BF16), place_ref[...], preferred_element_type=F32)
          + jnp.dot(c_hi, e_ref[0], preferred_element_type=F32)
          + jnp.dot(c_mid, e_ref[1], preferred_element_type=F32)
          + jnp.dot(c_lo, e_ref[2], preferred_element_type=F32))
    ka_ref[...] = ka.astype(BF16)


def _attn_prep(f, b_pad, proj3, tc):
    batch, t, _ = proj3.shape
    n = batch * t
    nt = t // tc
    w = W_MIX
    wa = N_HEADS * LANES
    src = jnp.arange(w, dtype=jnp.int32)
    dst = jnp.arange(wa, dtype=jnp.int32)
    place = (dst[None, :] == (src[:, None] // HEAD_DIM) * LANES + src[:, None] % HEAD_DIM).astype(BF16)
    lane = dst % LANES
    ones_row = ((lane >= HEAD_DIM) & (lane < HEAD_DIM + 3)).astype(F32)[None, :]
    hsrc = jnp.arange(LANES, dtype=jnp.int32)
    e = jnp.stack([-((dst[None, :] == hsrc[:, None] * LANES + HEAD_DIM + part) & (hsrc[:, None] < N_HEADS)).astype(F32)
                   for part in range(3)]).astype(BF16)
    return pl.pallas_call(
        _attn_prep_kernel,
        out_shape=(jax.ShapeDtypeStruct((n, LANES), F32), jax.ShapeDtypeStruct((batch, t, wa), BF16),
                   jax.ShapeDtypeStruct((batch, t, wa), BF16)),
        grid=(batch, nt),
        in_specs=[
            pl.BlockSpec((tc, LANES), lambda b, i: (b * nt + i, 0)),
            pl.BlockSpec((1, LANES), lambda b, i: (0, 0)),
            pl.BlockSpec((None, tc, w), lambda b, i: (b, i, COL_Q)),
            pl.BlockSpec((None, tc, w), lambda b, i: (b, i, COL_K)),
            pl.BlockSpec((w, wa), lambda b, i: (0, 0)),
            pl.BlockSpec((1, wa), lambda b, i: (0, 0)),
            pl.BlockSpec((3, LANES, wa), lambda b, i: (0, 0, 0)),
        ],
        out_specs=(
            pl.BlockSpec((tc, LANES), lambda b, i: (b * nt + i, 0)),
            pl.BlockSpec((None, tc, wa), lambda b, i: (b, i, 0)),
            pl.BlockSpec((None, tc, wa), lambda b, i: (b, i, 0)),
        ),
        scratch_shapes=[pltpu.VMEM((1, LANES), F32)],
        compiler_params=_params("parallel", "arbitrary"),
        name="attn_prep",
    )(f, b_pad, proj3, proj3, place, ones_row, e)


def _rglru_terms(xc, wr_ref, wi_ref, br_ref, bi_ref, lam_ref):
    r = jax.nn.sigmoid(_bdot(xc, wr_ref[...]) + br_ref[...])
    i = jax.nn.sigmoid(_bdot(xc, wi_ref[...]) + bi_ref[...])
    log_a = -LRU_C * r * _softplus(-lam_ref[...])
    a = jnp.exp(log_a)
    bterm = jnp.sqrt(-jnp.tanh(log_a) * (1.0 + a * a)) * i * xc
    return a, bterm


def _mixers_prompt_kernel(ba_ref, ca_ref, xa_ref, xc_ref, gc_ref, caw_ref, rcw_ref, rcb_ref,
                          wr_ref, wi_ref, br_ref, bi_ref, lam_ref,
                          ya_ref, yc_ref, su_ref, sx_ref, sh_ref, u_tail, x_tail, h_carry):
    @pl.when(pl.program_id(1) == 0)
    def _():
        u_tail[...] = jnp.zeros_like(u_tail)
        x_tail[...] = jnp.zeros_like(x_tail)
        h_carry[...] = jnp.zeros_like(h_carry)

    tt = ba_ref.shape[0]
    u = ca_ref[...] * xa_ref[...]
    ut = u_tail[...]
    ua = caw_ref[0:1, :] * _shift_rows(u, 2, ut) + caw_ref[1:2, :] * _shift_rows(u, 1, ut) + caw_ref[2:3, :] * u
    ya_ref[...] = (ba_ref[...] * ua).astype(BF16)
    u_tail[...] = u[tt - SUBLANES:, :]
    su_ref[...] = u[tt - SUBLANES:, :]
    x = xc_ref[...]
    xt = x_tail[...]
    xc = (rcw_ref[0:1, :] * _shift_rows(x, 3, xt) + rcw_ref[1:2, :] * _shift_rows(x, 2, xt)
          + rcw_ref[2:3, :] * _shift_rows(x, 1, xt) + rcw_ref[3:4, :] * x + rcb_ref[...])
    x_tail[...] = x[tt - SUBLANES:, :]
    sx_ref[...] = x[tt - SUBLANES:, :]
    a, b = _rglru_terms(xc, wr_ref, wi_ref, br_ref, bi_ref, lam_ref)
    row = lax.broadcasted_iota(jnp.int32, a.shape, 0)
    s = 1
    while s < tt:
        keep = row >= s
        a_prev = jnp.where(keep, pltpu.roll(a, s, axis=0), 1.0)
        b_prev = jnp.where(keep, pltpu.roll(b, s, axis=0), 0.0)
        b = a * b_prev + b
        a = a * a_prev
        s *= 2
    h = a * h_carry[...] + b
    h_carry[...] = h[tt - 1:tt, :]
    sh_ref[...] = h[tt - 1:tt, :]
    yc_ref[...] = (h * jax.nn.gelu(gc_ref[...])).astype(BF16)


def _mixers_prompt(proj3, caw, rcw, rcb, wr, wi, br, bi, lam, tt):
    batch, t, _ = proj3.shape
    w = W_MIX
    nt = t // tt

    def col(c):
        return pl.BlockSpec((None, tt, w), lambda b, i, c=c: (b, i, c))

    def const(shape):
        return pl.BlockSpec(shape, lambda b, i: (0,) * len(shape))

    return pl.pallas_call(
        _mixers_prompt_kernel,
        out_shape=(
            jax.ShapeDtypeStruct((batch, t, w), BF16),
            jax.ShapeDtypeStruct((batch, t, w), BF16),
            jax.ShapeDtypeStruct((batch, SUBLANES, w), F32),
            jax.ShapeDtypeStruct((batch, SUBLANES, w), F32),
            jax.ShapeDtypeStruct((batch, 1, w), F32),
        ),
        grid=(batch, nt),
        in_specs=[col(COL_BA), col(COL_CA), col(COL_XA), col(COL_XC), col(COL_GC),
                  const((3, w)), const((4, w)), const((1, w)), const((w, w)), const((w, w)),
                  const((1, w)), const((1, w)), const((1, w))],
        out_specs=(
            pl.BlockSpec((None, tt, w), lambda b, i: (b, i, 0)),
            pl.BlockSpec((None, tt, w), lambda b, i: (b, i, 0)),
            pl.BlockSpec((None, SUBLANES, w), lambda b, i: (b, 0, 0)),
            pl.BlockSpec((None, SUBLANES, w), lambda b, i: (b, 0, 0)),
            pl.BlockSpec((None, 1, w), lambda b, i: (b, 0, 0)),
        ),
        scratch_shapes=[pltpu.VMEM((SUBLANES, w), F32), pltpu.VMEM((SUBLANES, w), F32), pltpu.VMEM((1, w), F32)],
        compiler_params=_params("parallel", "arbitrary"),
        name="mixers_prompt",
    )(proj3, proj3, proj3, proj3, proj3, caw, rcw, rcb, wr, wi, br, bi, lam)


def _mixers_sample_kernel(ba_ref, ca_ref, xa_ref, xc_ref, gc_ref, ha0_ref, ha1_ref, hc0_ref, hc1_ref, hc2_ref, h0_ref,
                          caw_ref, rcw_ref, rcb_ref, wr_ref, wi_ref, br_ref, bi_ref, lam_ref, f_ref, bf_ref,
                          ya_ref, yc_ref, u_ref, h_ref, lf_ref):
    lf_ref[...] = _log_sigmoid(f_ref[...] + bf_ref[...])
    u = ca_ref[...] * xa_ref[...]
    ua = caw_ref[0:1, :] * ha0_ref[...] + caw_ref[1:2, :] * ha1_ref[...] + caw_ref[2:3, :] * u
    ya_ref[...] = (ba_ref[...] * ua).astype(BF16)
    u_ref[...] = u
    xc = (rcw_ref[0:1, :] * hc0_ref[...] + rcw_ref[1:2, :] * hc1_ref[...] + rcw_ref[2:3, :] * hc2_ref[...]
          + rcw_ref[3:4, :] * xc_ref[...] + rcb_ref[...])
    a, b = _rglru_terms(xc, wr_ref, wi_ref, br_ref, bi_ref, lam_ref)
    h = a * h0_ref[...] + b
    h_ref[...] = h
    yc_ref[...] = (h * jax.nn.gelu(gc_ref[...])).astype(BF16)


def _mixers_sample(proj, ha0, ha1, hc0, hc1, hc2, h0, caw, rcw, rcb, wr, wi, br, bi, lam, f, bf):
    n = proj.shape[0]
    w = W_MIX

    def col(c):
        return pl.BlockSpec((n, w), lambda i, c=c: (0, c))

    def full(shape):
        return pl.BlockSpec(shape, lambda i: (0,) * len(shape))

    return pl.pallas_call(
        _mixers_sample_kernel,
        out_shape=(jax.ShapeDtypeStruct((n, w), BF16), jax.ShapeDtypeStruct((n, w), BF16),
                   jax.ShapeDtypeStruct((n, w), F32), jax.ShapeDtypeStruct((n, w), F32),
                   jax.ShapeDtypeStruct((n, LANES), F32)),
        grid=(1,),
        in_specs=[col(COL_BA), col(COL_CA), col(COL_XA), col(COL_XC), col(COL_GC)]
        + [full((n, w))] * 6
        + [full((3, w)), full((4, w)), full((1, w)), full((w, w)), full((w, w)), full((1, w)), full((1, w)), full((1, w)),
           full((n, LANES)), full((1, LANES))],
        out_specs=(full((n, w)), full((n, w)), full((n, w)), full((n, w)), full((n, LANES))),
        compiler_params=_params("arbitrary"),
        name="mixers_sample",
    )(proj, proj, proj, proj, proj, ha0, ha1, hc0, hc1, hc2, h0, caw, rcw, rcb, wr, wi, br, bi, lam, f, bf)


def _flash_kernel(qi_ref, ki_ref, qa_ref, ka_ref, vt_ref, o_ref, m_s, l_s, acc):
    p = pl.program_id(1)
    qi = qi_ref[p]
    ki = ki_ref[p]
    tq = qa_ref.shape[0]
    tk = ka_ref.shape[0]

    @pl.when(ki == 0)
    def _():
        m_s[...] = jnp.full_like(m_s, NEG)
        l_s[...] = jnp.zeros_like(l_s)
        acc[...] = jnp.zeros_like(acc)

    def step(diagonal):
        if diagonal:
            krow = lax.broadcasted_iota(jnp.int32, (tk, tq), 0)
            qcol = lax.broadcasted_iota(jnp.int32, (tk, tq), 1)
            causal = krow <= qcol
        for h in range(N_HEADS):
            la = slice(h * LANES, (h + 1) * LANES)
            rows = slice(h * HEAD_DIM, (h + 1) * HEAD_DIM)
            st = lax.dot_general(ka_ref[:, la], qa_ref[:, la], (((1,), (1,)), ((), ())),
                                 preferred_element_type=F32)
            if diagonal:
                st = jnp.where(causal, st, NEG)
            m_prev = m_s[h:h + 1, :]
            m_new = jnp.maximum(m_prev, jnp.max(st, axis=0, keepdims=True))
            alpha = jnp.exp(m_prev - m_new)
            pt = jnp.exp(st - m_new)
            l_s[h:h + 1, :] = alpha * l_s[h:h + 1, :] + jnp.sum(pt, axis=0, keepdims=True)
            acc[rows, :] = alpha * acc[rows, :] + jnp.dot(vt_ref[rows, :], pt.astype(BF16),
                                                          preferred_element_type=F32)
            m_s[h:h + 1, :] = m_new

    @pl.when(ki < qi)
    def _():
        step(False)

    @pl.when(ki == qi)
    def _():
        step(True)
        for h in range(N_HEADS):
            rows = slice(h * HEAD_DIM, (h + 1) * HEAD_DIM)
            acc[rows, :] = acc[rows, :] / l_s[h:h + 1, :]
        o_ref[...] = acc[...].T.astype(BF16)


def _flash(qaug, kaug, vt, tq):
    batch, t, wa = qaug.shape
    w = W_MIX
    nq = t // tq
    pairs = [(i, j) for i in range(nq) for j in range(i + 1)]
    qi_tab = jnp.asarray([i for i, _ in pairs], dtype=jnp.int32)
    ki_tab = jnp.asarray([j for _, j in pairs], dtype=jnp.int32)
    grid_spec = pltpu.PrefetchScalarGridSpec(
        num_scalar_prefetch=2,
        grid=(batch, len(pairs)),
        in_specs=[
            pl.BlockSpec((None, tq, wa), lambda b, p, qi, ki: (b, qi[p], 0)),
            pl.BlockSpec((None, tq, wa), lambda b, p, qi, ki: (b, ki[p], 0)),
            pl.BlockSpec((w, tq), lambda b, p, qi, ki: (0, b * nq + ki[p])),
        ],
        out_specs=pl.BlockSpec((None, tq, w), lambda b, p, qi, ki: (b, qi[p], 0)),
        scratch_shapes=[pltpu.VMEM((N_HEADS, tq), F32), pltpu.VMEM((N_HEADS, tq), F32), pltpu.VMEM((w, tq), F32)],
    )
    return pl.pallas_call(
        _flash_kernel,
        out_shape=jax.ShapeDtypeStruct((batch, t, w), BF16),
        grid_spec=grid_spec,
        compiler_params=_params("parallel", "arbitrary"),
        name="flash_prompt",
    )(qi_tab, ki_tab, qaug, kaug, vt)


def _page_cumsum_kernel(lf_ref, m_ref, w_ref):
    x = lf_ref[...]
    hi = x.astype(BF16)
    r1 = x - hi.astype(F32)
    mid = r1.astype(BF16)
    lo = (r1 - mid.astype(F32)).astype(BF16)
    m = m_ref[...]
    w_ref[...] = (jnp.dot(hi, m, preferred_element_type=F32) + jnp.dot(mid, m, preferred_element_type=F32)
                  + jnp.dot(lo, m, preferred_element_type=F32))


def _page_cumsum(lf_flat, tm):
    n, pw = lf_flat.shape
    idx = jnp.arange(pw, dtype=jnp.int32)
    tri = ((idx[:, None] % N_HEADS == idx[None, :] % N_HEADS) & (idx[:, None] <= idx[None, :])).astype(BF16)
    return pl.pallas_call(
        _page_cumsum_kernel,
        out_shape=jax.ShapeDtypeStruct((n, pw), F32),
        grid=(n // tm,),
        in_specs=[pl.BlockSpec((tm, pw), lambda i: (i, 0)), pl.BlockSpec((pw, pw), lambda i: (0, 0))],
        out_specs=pl.BlockSpec((tm, pw), lambda i: (i, 0)),
        compiler_params=_params("parallel"),
        name="page_cumsum",
    )(lf_flat, tri)


def _paged_attn_kernel(pt_ref, q_ref, kn_ref, vn_ref, lfn_ref, *refs, n_pages):
    k_refs = refs[:n_pages]
    v_refs = refs[n_pages:2 * n_pages]
    w_refs = refs[2 * n_pages:3 * n_pages]
    o_ref = refs[3 * n_pages]
    pw = w_refs[0].shape[-1]
    gsz = N_HEADS

    qb = (q_ref[...] * (HEAD_DIM ** -0.5)).astype(BF16)
    wm = jnp.concatenate([w[...] for w in w_refs], axis=0)
    lane = lax.broadcasted_iota(jnp.int32, wm.shape, 1)
    tot = jnp.where(lane >= pw - gsz, wm, 0.0)
    s = gsz
    while s < pw:
        tot = tot + pltpu.roll(tot, s, axis=1)
        s *= 2
    prow = lax.broadcasted_iota(jnp.int32, wm.shape, 0)
    off = tot
    s = 1
    while s < n_pages:
        off = off + jnp.where(prow >= s, pltpu.roll(off, s, axis=0), 0.0)
        s *= 2
    c_flat = wm + (off - tot)
    c_new = off[n_pages - 1:n_pages, 0:gsz] + lfn_ref[:, 0:gsz]

    sub = lax.broadcasted_iota(jnp.int32, (N_HEADS, pw), 0)
    lanep = lax.broadcasted_iota(jnp.int32, (N_HEADS, pw), 1)
    own = (lanep & (N_HEADS - 1)) == sub
    zs = []
    for j in range(n_pages):
        kj = k_refs[j][...].reshape(pw, HEAD_DIM).astype(BF16)
        sj = lax.dot_general(qb, kj, (((1,), (1,)), ((), ())), preferred_element_type=F32)
        zs.append(jnp.where(own, sj - c_flat[j:j + 1, :], NEG))
    sn = lax.dot_general(qb, kn_ref[...].astype(BF16), (((1,), (1,)), ((), ())), preferred_element_type=F32)
    own_n = (lax.broadcasted_iota(jnp.int32, sn.shape, 0) == lax.broadcasted_iota(jnp.int32, sn.shape, 1))
    zn = jnp.where(own_n, sn - c_new, NEG)
    m = jnp.max(zn, axis=-1, keepdims=True)
    for z in zs:
        m = jnp.maximum(m, jnp.max(z, axis=-1, keepdims=True))
    pn = jnp.exp(zn - m)
    l = jnp.sum(pn, axis=-1, keepdims=True)
    out = jnp.dot(pn.astype(BF16), vn_ref[...].astype(BF16), preferred_element_type=F32)
    for j in range(n_pages):
        p = jnp.exp(zs[j] - m)
        l = l + jnp.sum(p, axis=-1, keepdims=True)
        vj = v_refs[j][...].reshape(pw, HEAD_DIM).astype(BF16)
        out = out + jnp.dot(p.astype(BF16), vj, preferred_element_type=F32)
    o_ref[...] = out / l


def _paged_attn(page_table, q3, kn3, vn3, lfn, cache_k, cache_v, wpage, layer):
    n = q3.shape[0]
    n_pages = page_table.shape[1]
    n_phys, page = cache_k.shape[1], cache_k.shape[2]
    pw = page * N_HEADS
    row3 = pl.BlockSpec((None, N_HEADS, HEAD_DIM), lambda b, pt: (b, 0, 0))

    def kv(j):
        return pl.BlockSpec((None, None, page, N_HEADS, HEAD_DIM), lambda b, pt, j=j: (layer, pt[b * n_pages + j], 0, 0, 0))

    def wp(j):
        return pl.BlockSpec((None, 1, pw), lambda b, pt, j=j: (layer * n_phys + pt[b * n_pages + j], 0, 0))

    grid_spec = pltpu.PrefetchScalarGridSpec(
        num_scalar_prefetch=1,
        grid=(n,),
        in_specs=[row3, row3, row3, pl.BlockSpec((None, 1, LANES), lambda b, pt: (b, 0, 0))]
        + [kv(j) for j in range(n_pages)] + [kv(j) for j in range(n_pages)] + [wp(j) for j in range(n_pages)],
        out_specs=row3,
    )
    return pl.pallas_call(
        functools.partial(_paged_attn_kernel, n_pages=n_pages),
        out_shape=jax.ShapeDtypeStruct((n, N_HEADS, HEAD_DIM), F32),
        grid_spec=grid_spec,
        compiler_params=_params("arbitrary"),
        name="paged_attn",
    )(page_table.reshape(-1), q3, kn3, vn3, lfn, *([cache_k] * n_pages), *([cache_v] * n_pages), *([wpage] * n_pages))


def _merge_kernel(x_ref, ya_ref, yb_ref, yc_ref, g0_ref, g1_ref, g2_ref, wb_ref, wo_ref, o_ref):
    w = W_MIX
    merged = (jax.nn.sigmoid(g0_ref[...]) * jnp.dot(ya_ref[...], wb_ref[0:w, :], preferred_element_type=F32)
              + jax.nn.sigmoid(g1_ref[...]) * jnp.dot(yb_ref[...], wb_ref[w:2 * w, :], preferred_element_type=F32)
              + jax.nn.sigmoid(g2_ref[...]) * jnp.dot(yc_ref[...], wb_ref[2 * w:3 * w, :], preferred_element_type=F32))
    o_ref[...] = x_ref[...] + jnp.dot(merged.astype(BF16), wo_ref[...], preferred_element_type=F32)


def _merge(x, ya, yb, yc, proj, wb, wo, tm):
    n, d = x.shape
    w = W_MIX

    def rows(width, c=0):
        return pl.BlockSpec((tm, width), lambda i, c=c: (i, c))

    return pl.pallas_call(
        _merge_kernel,
        out_shape=jax.ShapeDtypeStruct((n, d), F32),
        grid=(n // tm,),
        in_specs=[rows(d), rows(w), rows(w), rows(w), rows(d, COL_GATES), rows(d, COL_GATES + 1), rows(d, COL_GATES + 2),
                  pl.BlockSpec((3 * w, d), lambda i: (0, 0)), pl.BlockSpec((d, d), lambda i: (0, 0))],
        out_specs=rows(d),
        compiler_params=_params("parallel"),
        name="merge",
    )(x, ya, yb, yc, proj, proj, proj, wb, wo)


def _ffn_kernel(*refs, chunk, hist, final, tiles_per_seq):
    if hist:
        x_ref, g_ref, wup_ref, cw_ref, cb_ref, wdn_ref, h0_ref, h1_ref = refs[:8]
        refs = refs[8:]
    else:
        x_ref, g_ref, wup_ref, cw_ref, cb_ref, wdn_ref = refs[:6]
        refs = refs[6:]
    if final:
        gf_ref = refs[0]
        refs = refs[1:]
    if hist:
        o_ref, st_ref = refs
    else:
        o_ref, st_ref, tail = refs

        @pl.when(pl.program_id(0) % tiles_per_seq == 0)
        def _():
            tail[...] = jnp.zeros_like(tail)

    x = x_ref[...]
    tm = x.shape[0]
    dff = wdn_ref.shape[0]
    hb = _rmsnorm(x, g_ref[...]).astype(BF16)
    acc = jnp.zeros(x.shape, F32)
    for c in range(dff // chunk):
        halves = []
        for off in (c * chunk, dff + c * chunk):
            sl = slice(off, off + chunk)
            up = jnp.dot(hb, wup_ref[:, sl], preferred_element_type=F32)
            if hist:
                prev2, prev1 = h0_ref[:, sl], h1_ref[:, sl]
                st_ref[:, sl] = up
            else:
                tl = tail[:, sl]
                prev2, prev1 = _shift_rows(up, 2, tl), _shift_rows(up, 1, tl)
                tail[:, sl] = up[tm - SUBLANES:, :]
                st_ref[:, sl] = up[tm - SUBLANES:, :]
            halves.append(cw_ref[0:1, sl] * prev2 + cw_ref[1:2, sl] * prev1 + cw_ref[2:3, sl] * up + cb_ref[:, sl])
        act = jax.nn.gelu(halves[0]) * halves[1]
        acc = acc + jnp.dot(act.astype(BF16), wdn_ref[c * chunk:(c + 1) * chunk, :], preferred_element_type=F32)
    y = x + acc
    if final:
        y = _rmsnorm(y, gf_ref[...])
    o_ref[...] = y


def _ffn(x, g, wup, cw, cb, wdn, tm, chunk, hist=None, final_g=None, tiles_per_seq=1):
    n, d = x.shape
    dup = wup.shape[1]
    nt = n // tm
    resident = dict(pipeline_mode=pl.Buffered(1))
    in_specs = [
        pl.BlockSpec((tm, d), lambda i: (i, 0)),
        pl.BlockSpec((1, d), lambda i: (0, 0)),
        pl.BlockSpec((d, dup), lambda i: (0, 0), **resident),
        pl.BlockSpec((3, dup), lambda i: (0, 0)),
        pl.BlockSpec((1, dup), lambda i: (0, 0)),
        pl.BlockSpec((dup // 2, d), lambda i: (0, 0), **resident),
    ]
    args = [x, g, wup, cw, cb, wdn]
    scratch = []
    if hist is not None:
        in_specs += [pl.BlockSpec((tm, dup), lambda i: (i, 0))] * 2
        args += list(hist)
        st_shape = jax.ShapeDtypeStruct((n, dup), F32)
        st_spec = pl.BlockSpec((tm, dup), lambda i: (i, 0))
    else:
        nseq = nt // tiles_per_seq
        st_shape = jax.ShapeDtypeStruct((nseq, SUBLANES, dup), F32)
        st_spec = pl.BlockSpec((None, SUBLANES, dup), lambda i: (i // tiles_per_seq, 0, 0))
        scratch = [pltpu.VMEM((SUBLANES, dup), F32)]
    if final_g is not None:
        in_specs.append(pl.BlockSpec((1, d), lambda i: (0, 0)))
        args.append(final_g)
    return pl.pallas_call(
        functools.partial(_ffn_kernel, chunk=chunk, hist=hist is not None, final=final_g is not None,
                          tiles_per_seq=tiles_per_seq),
        out_shape=(jax.ShapeDtypeStruct((n, d), F32), st_shape),
        grid=(nt,),
        in_specs=in_specs,
        out_specs=(pl.BlockSpec((tm, d), lambda i: (i, 0)), st_spec),
        scratch_shapes=scratch,
        compiler_params=_params("arbitrary"),
        name="ffn",
    )(*args)


def _tile(n, want):
    if n <= want:
        return n
    t = want
    while n % t:
        t -= SUBLANES
    return t


def kernel(x_prompt, x_sample, cache_k, cache_v, cache_logf, page_table, state_conv_a, state_rg_conv, state_rg_h, state_ffn_conv, norm1_g, w_in, conv_a_w, b_forget, rg_conv_w, rg_conv_b, rg_w_r, rg_b_r, rg_w_i, rg_b_i, rg_lambda, w_branch, w_out, norm2_g, ffn_w_up, ffn_conv_w, ffn_conv_b, ffn_w_down, final_norm_g):
    batch, seq, d = x_prompt.shape
    nd = x_sample.shape[0]
    depth = w_in.shape[0]
    w = W_MIX
    n_phys, page = cache_k.shape[1], cache_k.shape[2]
    n_prompt = batch * seq
    f_lo = 6 * w

    w_main = jnp.concatenate([w_in[:, :, :f_lo], w_in[:, :, f_lo + N_HEADS:]], axis=-1).astype(BF16)
    w_f = jnp.pad(w_in[:, :, f_lo:f_lo + N_HEADS], ((0, 0), (0, 0), (0, LANES - N_HEADS))).astype(BF16)
    w_vt = jnp.swapaxes(w_in[:, :, COL_V * w:(COL_V + 1) * w], 1, 2).astype(BF16)
    b_f = jnp.pad(b_forget, ((0, 0), (0, LANES - N_HEADS)))[:, None, :]
    eye = jnp.eye(rg_w_r.shape[1], dtype=F32)
    wr_bd = jnp.einsum("lnde,nm->lndme", rg_w_r, eye).reshape(depth, w, w).astype(BF16)
    wi_bd = jnp.einsum("lnde,nm->lndme", rg_w_i, eye).reshape(depth, w, w).astype(BF16)
    wb = w_branch.astype(BF16)
    wo = w_out.astype(BF16)
    wup = ffn_w_up.astype(BF16)
    wdn = ffn_w_down.astype(BF16)
    row = lambda a: a[:, None, :]
    g1, g2, rcb, rbr, rbi, lam, fcb = map(row, (norm1_g, norm2_g, rg_conv_b, rg_b_r, rg_b_i, rg_lambda, ffn_conv_b))
    gf = final_norm_g[None, :]

    lf_flat = cache_logf.reshape(depth * n_phys, page * N_HEADS)
    wpage = _page_cumsum(lf_flat, _tile(depth * n_phys, 1024)).reshape(depth * n_phys, 1, page * N_HEADS)

    tm_p = _tile(n_prompt, 1024)
    tt = _tile(seq, 256)
    tq = _tile(seq, 512)
    tf = _tile(seq, 512)
    xp = x_prompt.reshape(n_prompt, d)
    xs = x_sample.reshape(nd, d)
    sp, ss = [], []
    for l in range(depth):
        last = l == depth - 1
        proj, f, vt = _inproj(xp, g1[l], w_main[l], w_f[l], tm_p, 1024, wvt=w_vt[l])
        proj3 = proj.reshape(batch, seq, proj.shape[1])
        logf, qaug, kaug = _attn_prep(f, b_f[l], proj3, _tile(seq, 1024))
        ya, yc, su, sx, sh = _mixers_prompt(proj3, conv_a_w[l], rg_conv_w[l], rcb[l], wr_bd[l], wi_bd[l],
                                            rbr[l], rbi[l], lam[l], tt)
        yb = _flash(qaug, kaug, vt, tq)
        x1 = _merge(xp, ya.reshape(n_prompt, w), yb.reshape(n_prompt, w), yc.reshape(n_prompt, w), proj, wb[l], wo[l],
                    _tile(n_prompt, 512))
        xp, sf = _ffn(x1, g2[l], wup[l], ffn_conv_w[l], fcb[l], wdn[l], tf, 512,
                      final_g=gf if last else None, tiles_per_seq=seq // tf)
        sp.append((su[:, SUBLANES - 2:], proj3[:, :, COL_K * w:(COL_K + 1) * w].reshape(batch, seq, N_HEADS, HEAD_DIM),
                   proj3[:, :, COL_V * w:(COL_V + 1) * w].reshape(batch, seq, N_HEADS, HEAD_DIM),
                   logf[:, :N_HEADS].reshape(batch, seq, N_HEADS), sx[:, SUBLANES - 3:], sh[:, 0], sf[:, SUBLANES - 2:]))
        projs, fs = _inproj(xs, g1[l], w_main[l], w_f[l], nd, 1024)
        ha, hc, hf = state_conv_a[l], state_rg_conv[l], state_ffn_conv[l]
        yas, ycs, us, hs, lfs = _mixers_sample(projs, ha[:, 0], ha[:, 1], hc[:, 0], hc[:, 1], hc[:, 2], state_rg_h[l],
                                               conv_a_w[l], rg_conv_w[l], rcb[l], wr_bd[l], wi_bd[l], rbr[l], rbi[l],
                                               lam[l], fs, b_f[l])
        q3 = projs[:, COL_Q * w:(COL_Q + 1) * w].reshape(nd, N_HEADS, HEAD_DIM)
        k3 = projs[:, COL_K * w:(COL_K + 1) * w].reshape(nd, N_HEADS, HEAD_DIM)
        v3 = projs[:, COL_V * w:(COL_V + 1) * w].reshape(nd, N_HEADS, HEAD_DIM)
        ybs = _paged_attn(page_table, q3, k3, v3, lfs[:, None, :], cache_k, cache_v, wpage, l)
        x1s = _merge(xs, yas, ybs.reshape(nd, w).astype(BF16), ycs, projs, wb[l], wo[l], nd)
        xs, ups = _ffn(x1s, g2[l], wup[l], ffn_conv_w[l], fcb[l], wdn[l], nd, 512, hist=(hf[:, 0], hf[:, 1]),
                       final_g=gf if last else None)
        ss.append((jnp.stack([ha[:, 1], us], axis=1), k3[:, None], v3[:, None], lfs[:, None, :N_HEADS],
                   jnp.stack([hc[:, 1], hc[:, 2], projs[:, COL_XC * w:(COL_XC + 1) * w]], axis=1), hs,
                   jnp.stack([hf[:, 1], ups], axis=1)))

    def stk(outs, i):
        return jnp.stack([o[i] for o in outs], axis=0)

    return (xp.reshape(batch, seq, d), xs.reshape(nd, 1, d),
            stk(sp, 0), stk(sp, 1), stk(sp, 2), stk(sp, 3), stk(sp, 4), stk(sp, 5), stk(sp, 6),
            stk(ss, 0), stk(ss, 1), stk(ss, 2), stk(ss, 3), stk(ss, 4), stk(ss, 5), stk(ss, 6))
```

```python
import functools

import jax
import jax.numpy as jnp
from jax import lax
from jax.experimental import pallas as pl
from jax.experimental.pallas import tpu as pltpu

F32 = jnp.float32
BF16 = jnp.bfloat16

EPS = 1e-6
LRU_C = 8.0
N_HEADS = 8
HEAD_DIM = 64
W_MIX = N_HEADS * HEAD_DIM
NEG = -1e30
LOG2E = 1.4426950408889634
FLASH_LOOKAHEAD = 2
LANES = 128
SUBLANES = 8
VMEM_LIMIT = 56 * 1024 * 1024

COL_BA, COL_CA, COL_XA, COL_Q, COL_K, COL_V, COL_XC, COL_GC = range(8)
COL_GATES = 4


def _params(*sem):
    return pltpu.CompilerParams(dimension_semantics=sem, vmem_limit_bytes=VMEM_LIMIT)


def _rmsnorm(x, g):
    ms = jnp.mean(x * x, axis=-1, keepdims=True)
    return x * lax.rsqrt(ms + EPS) * g


def _softplus(y):
    return jnp.maximum(y, 0.0) + jnp.log1p(jnp.exp(-jnp.abs(y)))


def _log_sigmoid(z):
    return -_softplus(-z)


def _bdot(a, b):
    return jnp.dot(a.astype(BF16), b.astype(BF16), preferred_element_type=F32)


def _split3(x):
    hi = x.astype(BF16)
    r1 = x - hi.astype(F32)
    mid = r1.astype(BF16)
    lo = (r1 - mid.astype(F32)).astype(BF16)
    return hi, mid, lo


def _shift_rows(x, k, tail):
    y = pltpu.roll(x, k, axis=0)
    row = lax.broadcasted_iota(jnp.int32, tail.shape, 0)
    first = jnp.where(row < k, pltpu.roll(tail, k, axis=0), y[:SUBLANES])
    return jnp.concatenate([first, y[SUBLANES:]], axis=0)


def _inproj_kernel(*refs, with_vt):
    if with_vt:
        x_ref, g_ref, w_ref, wf_ref, wvt_ref, proj_ref, f_ref, vt_ref, h_scr = refs
    else:
        x_ref, g_ref, w_ref, wf_ref, proj_ref, f_ref, h_scr = refs

    @pl.when(pl.program_id(1) == 0)
    def _():
        hb = _rmsnorm(x_ref[...], g_ref[...]).astype(BF16)
        h_scr[...] = hb
        f_ref[...] = jnp.dot(hb, wf_ref[...], preferred_element_type=F32)
        if with_vt:
            vt = lax.dot_general(wvt_ref[...], hb, (((1,), (1,)), ((), ())), preferred_element_type=F32)
            vt_ref[...] = vt.astype(BF16)

    proj_ref[...] = jnp.dot(h_scr[...], w_ref[...], preferred_element_type=F32)


def _inproj(x, g, w, wf, tm, tn, wvt=None):
    n, d = x.shape
    nw = w.shape[1]
    in_specs = [
        pl.BlockSpec((tm, d), lambda i, j: (i, 0)),
        pl.BlockSpec((1, d), lambda i, j: (0, 0)),
        pl.BlockSpec((d, tn), lambda i, j: (0, j)),
        pl.BlockSpec((d, LANES), lambda i, j: (0, 0)),
    ]
    out_shape = [jax.ShapeDtypeStruct((n, nw), F32), jax.ShapeDtypeStruct((n, LANES), F32)]
    out_specs = [pl.BlockSpec((tm, tn), lambda i, j: (i, j)), pl.BlockSpec((tm, LANES), lambda i, j: (i, 0))]
    args = [x, g, w, wf]
    if wvt is not None:
        in_specs.append(pl.BlockSpec((W_MIX, d), lambda i, j: (0, 0)))
        out_shape.append(jax.ShapeDtypeStruct((W_MIX, n), BF16))
        out_specs.append(pl.BlockSpec((W_MIX, tm), lambda i, j: (0, i)))
        args.append(wvt)
    return pl.pallas_call(
        functools.partial(_inproj_kernel, with_vt=wvt is not None),
        out_shape=tuple(out_shape),
        grid=(n // tm, nw // tn),
        in_specs=in_specs,
        out_specs=tuple(out_specs),
        scratch_shapes=[pltpu.VMEM((tm, d), BF16)],
        compiler_params=_params("parallel", "arbitrary"),
        name="inproj",
    )(*args)


def _attn_prep_kernel(f_ref, b_ref, q_ref, k_ref, place_ref, ones_ref, e_ref, logf_ref, qa_ref, ka_ref, carry):
    @pl.when(pl.program_id(1) == 0)
    def _():
        carry[...] = jnp.zeros_like(carry)

    lf = _log_sigmoid(f_ref[...] + b_ref[...])
    logf_ref[...] = lf
    tc = lf.shape[0]
    row = lax.broadcasted_iota(jnp.int32, lf.shape, 0)
    c = lf
    s = 1
    while s < tc:
        c = c + jnp.where(row >= s, pltpu.roll(c, s, axis=0), 0.0)
        s *= 2
    c = c + carry[...]
    carry[...] = c[tc - 1:tc, :]
    c_hi, c_mid, c_lo = _split3(c * LOG2E)
    qb = (q_ref[...] * (LOG2E * HEAD_DIM ** -0.5)).astype(BF16)
    qa_ref[...] = (jnp.dot(qb, place_ref[...], preferred_element_type=F32) + ones_ref[...]).astype(BF16)
    ka = (jnp.dot(k_ref[...].astype(BF16), place_ref[...], preferred_element_type=F32)
          + jnp.dot(c_hi, e_ref[0], preferred_element_type=F32)
          + jnp.dot(c_mid, e_ref[1], preferred_element_type=F32)
          + jnp.dot(c_lo, e_ref[2], preferred_element_type=F32))
    ka_ref[...] = ka.astype(BF16)


def _attn_prep(f, b_pad, proj3, tc):
    batch, t, _ = proj3.shape
    n = batch * t
    nt = t // tc
    w = W_MIX
    wa = N_HEADS * LANES
    src = jnp.arange(w, dtype=jnp.int32)
    dst = jnp.arange(wa, dtype=jnp.int32)
    place = (dst[None, :] == (src[:, None] // HEAD_DIM) * LANES + src[:, None] % HEAD_DIM).astype(BF16)
    lane = dst % LANES
    ones_row = ((lane >= HEAD_DIM) & (lane < HEAD_DIM + 3)).astype(F32)[None, :]
    hsrc = jnp.arange(LANES, dtype=jnp.int32)
    e = jnp.stack([-((dst[None, :] == hsrc[:, None] * LANES + HEAD_DIM + part) & (hsrc[:, None] < N_HEADS)).astype(F32)
                   for part in range(3)]).astype(BF16)
    return pl.pallas_call(
        _attn_prep_kernel,
        out_shape=(jax.ShapeDtypeStruct((n, LANES), F32), jax.ShapeDtypeStruct((batch, t, wa), BF16),
                   jax.ShapeDtypeStruct((batch, t, wa), BF16)),
        grid=(batch, nt),
        in_specs=[
            pl.BlockSpec((tc, LANES), lambda b, i: (b * nt + i, 0)),
            pl.BlockSpec((1, LANES), lambda b, i: (0, 0)),
            pl.BlockSpec((None, tc, w), lambda b, i: (b, i, COL_Q)),
            pl.BlockSpec((None, tc, w), lambda b, i: (b, i, COL_K)),
            pl.BlockSpec((w, wa), lambda b, i: (0, 0)),
            pl.BlockSpec((1, wa), lambda b, i: (0, 0)),
            pl.BlockSpec((3, LANES, wa), lambda b, i: (0, 0, 0)),
        ],
        out_specs=(
            pl.BlockSpec((tc, LANES), lambda b, i: (b * nt + i, 0)),
            pl.BlockSpec((None, tc, wa), lambda b, i: (b, i, 0)),
            pl.BlockSpec((None, tc, wa), lambda b, i: (b, i, 0)),
        ),
        scratch_shapes=[pltpu.VMEM((1, LANES), F32)],
        compiler_params=_params("parallel", "arbitrary"),
        name="attn_prep",
    )(f, b_pad, proj3, proj3, place, ones_row, e)


def _rglru_terms(xc, wr_ref, wi_ref, br_ref, bi_ref, lam_ref):
    r = jax.nn.sigmoid(_bdot(xc, wr_ref[...]) + br_ref[...])
    i = jax.nn.sigmoid(_bdot(xc, wi_ref[...]) + bi_ref[...])
    log_a = -LRU_C * r * _softplus(-lam_ref[...])
    a = jnp.exp(log_a)
    bterm = jnp.sqrt(-jnp.tanh(log_a) * (1.0 + a * a)) * i * xc
    return a, bterm


def _mixers_prompt_kernel(ba_ref, ca_ref, xa_ref, xc_ref, gc_ref, caw_ref, rcw_ref, rcb_ref,
                          wr_ref, wi_ref, br_ref, bi_ref, lam_ref,
                          ya_ref, yc_ref, su_ref, sx_ref, sh_ref, u_tail, x_tail, h_carry):
    @pl.when(pl.program_id(1) == 0)
    def _():
        u_tail[...] = jnp.zeros_like(u_tail)
        x_tail[...] = jnp.zeros_like(x_tail)
        h_carry[...] = jnp.zeros_like(h_carry)

    tt = ba_ref.shape[0]
    u = ca_ref[...] * xa_ref[...]
    ut = u_tail[...]
    ua = caw_ref[0:1, :] * _shift_rows(u, 2, ut) + caw_ref[1:2, :] * _shift_rows(u, 1, ut) + caw_ref[2:3, :] * u
    ya_ref[...] = (ba_ref[...] * ua).astype(BF16)
    u_tail[...] = u[tt - SUBLANES:, :]
    su_ref[...] = u[tt - SUBLANES:, :]
    x = xc_ref[...]
    xt = x_tail[...]
    xc = (rcw_ref[0:1, :] * _shift_rows(x, 3, xt) + rcw_ref[1:2, :] * _shift_rows(x, 2, xt)
          + rcw_ref[2:3, :] * _shift_rows(x, 1, xt) + rcw_ref[3:4, :] * x + rcb_ref[...])
    x_tail[...] = x[tt - SUBLANES:, :]
    sx_ref[...] = x[tt - SUBLANES:, :]
    a, b = _rglru_terms(xc, wr_ref, wi_ref, br_ref, bi_ref, lam_ref)
    row = lax.broadcasted_iota(jnp.int32, a.shape, 0)
    s = 1
    while s < tt:
        keep = row >= s
        a_prev = jnp.where(keep, pltpu.roll(a, s, axis=0), 1.0)
        b_prev = jnp.where(keep, pltpu.roll(b, s, axis=0), 0.0)
        b = a * b_prev + b
        a = a * a_prev
        s *= 2
    h = a * h_carry[...] + b
    h_carry[...] = h[tt - 1:tt, :]
    sh_ref[...] = h[tt - 1:tt, :]
    yc_ref[...] = (h * jax.nn.gelu(gc_ref[...])).astype(BF16)


def _mixers_prompt(proj3, caw, rcw, rcb, wr, wi, br, bi, lam, tt):
    batch, t, _ = proj3.shape
    w = W_MIX
    nt = t // tt

    def col(c):
        return pl.BlockSpec((None, tt, w), lambda b, i, c=c: (b, i, c))

    def const(shape):
        return pl.BlockSpec(shape, lambda b, i: (0,) * len(shape))

    return pl.pallas_call(
        _mixers_prompt_kernel,
        out_shape=(
            jax.ShapeDtypeStruct((batch, t, w), BF16),
            jax.ShapeDtypeStruct((batch, t, w), BF16),
            jax.ShapeDtypeStruct((batch, SUBLANES, w), F32),
            jax.ShapeDtypeStruct((batch, SUBLANES, w), F32),
            jax.ShapeDtypeStruct((batch, 1, w), F32),
        ),
        grid=(batch, nt),
        in_specs=[col(COL_BA), col(COL_CA), col(COL_XA), col(COL_XC), col(COL_GC),
                  const((3, w)), const((4, w)), const((1, w)), const((w, w)), const((w, w)),
                  const((1, w)), const((1, w)), const((1, w))],
        out_specs=(
            pl.BlockSpec((None, tt, w), lambda b, i: (b, i, 0)),
            pl.BlockSpec((None, tt, w), lambda b, i: (b, i, 0)),
            pl.BlockSpec((None, SUBLANES, w), lambda b, i: (b, 0, 0)),
            pl.BlockSpec((None, SUBLANES, w), lambda b, i: (b, 0, 0)),
            pl.BlockSpec((None, 1, w), lambda b, i: (b, 0, 0)),
        ),
        scratch_shapes=[pltpu.VMEM((SUBLANES, w), F32), pltpu.VMEM((SUBLANES, w), F32), pltpu.VMEM((1, w), F32)],
        compiler_params=_params("parallel", "arbitrary"),
        name="mixers_prompt",
    )(proj3, proj3, proj3, proj3, proj3, caw, rcw, rcb, wr, wi, br, bi, lam)


def _mixers_sample_kernel(ba_ref, ca_ref, xa_ref, xc_ref, gc_ref, ha0_ref, ha1_ref, hc0_ref, hc1_ref, hc2_ref, h0_ref,
                          caw_ref, rcw_ref, rcb_ref, wr_ref, wi_ref, br_ref, bi_ref, lam_ref, f_ref, bf_ref,
                          ya_ref, yc_ref, u_ref, h_ref, lf_ref):
    lf_ref[...] = _log_sigmoid(f_ref[...] + bf_ref[...])
    u = ca_ref[...] * xa_ref[...]
    ua = caw_ref[0:1, :] * ha0_ref[...] + caw_ref[1:2, :] * ha1_ref[...] + caw_ref[2:3, :] * u
    ya_ref[...] = (ba_ref[...] * ua).astype(BF16)
    u_ref[...] = u
    xc = (rcw_ref[0:1, :] * hc0_ref[...] + rcw_ref[1:2, :] * hc1_ref[...] + rcw_ref[2:3, :] * hc2_ref[...]
          + rcw_ref[3:4, :] * xc_ref[...] + rcb_ref[...])
    a, b = _rglru_terms(xc, wr_ref, wi_ref, br_ref, bi_ref, lam_ref)
    h = a * h0_ref[...] + b
    h_ref[...] = h
    yc_ref[...] = (h * jax.nn.gelu(gc_ref[...])).astype(BF16)


def _mixers_sample(proj, ha0, ha1, hc0, hc1, hc2, h0, caw, rcw, rcb, wr, wi, br, bi, lam, f, bf):
    n = proj.shape[0]
    w = W_MIX

    def col(c):
        return pl.BlockSpec((n, w), lambda i, c=c: (0, c))

    def full(shape):
        return pl.BlockSpec(shape, lambda i: (0,) * len(shape))

    return pl.pallas_call(
        _mixers_sample_kernel,
        out_shape=(jax.ShapeDtypeStruct((n, w), BF16), jax.ShapeDtypeStruct((n, w), BF16),
                   jax.ShapeDtypeStruct((n, w), F32), jax.ShapeDtypeStruct((n, w), F32),
                   jax.ShapeDtypeStruct((n, LANES), F32)),
        grid=(1,),
        in_specs=[col(COL_BA), col(COL_CA), col(COL_XA), col(COL_XC), col(COL_GC)]
        + [full((n, w))] * 6
        + [full((3, w)), full((4, w)), full((1, w)), full((w, w)), full((w, w)), full((1, w)), full((1, w)), full((1, w)),
           full((n, LANES)), full((1, LANES))],
        out_specs=(full((n, w)), full((n, w)), full((n, w)), full((n, w)), full((n, LANES))),
        compiler_params=_params("arbitrary"),
        name="mixers_sample",
    )(proj, proj, proj, proj, proj, ha0, ha1, hc0, hc1, hc2, h0, caw, rcw, rcb, wr, wi, br, bi, lam, f, bf)


def _flash_kernel(qi_ref, ki_ref, qa_ref, ka_ref, vt_ref, o_ref, m_s, l_s, acc):
    p = pl.program_id(1)
    qi = qi_ref[p]
    ki = ki_ref[p]
    tq = qa_ref.shape[0]
    tk = ka_ref.shape[0]

    @pl.when(ki == 0)
    def _():
        m_s[...] = jnp.full_like(m_s, NEG)
        l_s[...] = jnp.zeros_like(l_s)
        acc[...] = jnp.zeros_like(acc)

    def step(diagonal):
        if diagonal:
            krow = lax.broadcasted_iota(jnp.int32, (tk, tq), 0)
            qcol = lax.broadcasted_iota(jnp.int32, (tk, tq), 1)
            causal = krow <= qcol

        def scores(h):
            la = slice(h * LANES, (h + 1) * LANES)
            return lax.dot_general(ka_ref[:, la], qa_ref[:, la], (((1,), (1,)), ((), ())),
                                   preferred_element_type=F32)

        sts = [scores(h) for h in range(FLASH_LOOKAHEAD)]
        for h in range(N_HEADS):
            rows = slice(h * HEAD_DIM, (h + 1) * HEAD_DIM)
            if h + FLASH_LOOKAHEAD < N_HEADS:
                sts.append(scores(h + FLASH_LOOKAHEAD))
            st = sts[h]
            if diagonal:
                st = jnp.where(causal, st, NEG)
            m_prev = m_s[h:h + 1, :]
            m_new = jnp.maximum(m_prev, jnp.max(st, axis=0, keepdims=True))
            alpha = jnp.exp2(m_prev - m_new)
            pt = jnp.exp2(st - m_new)
            l_s[h:h + 1, :] = alpha * l_s[h:h + 1, :] + jnp.sum(pt, axis=0, keepdims=True)
            acc[rows, :] = alpha * acc[rows, :] + jnp.dot(vt_ref[rows, :], pt.astype(BF16),
                                                          preferred_element_type=F32)
            m_s[h:h + 1, :] = m_new

    @pl.when(ki < qi)
    def _():
        step(False)

    @pl.when(ki == qi)
    def _():
        step(True)
        for h in range(N_HEADS):
            rows = slice(h * HEAD_DIM, (h + 1) * HEAD_DIM)
            acc[rows, :] = acc[rows, :] / l_s[h:h + 1, :]
        o_ref[...] = acc[...].T.astype(BF16)


def _flash(qaug, kaug, vt, tq):
    batch, t, wa = qaug.shape
    w = W_MIX
    nq = t // tq
    pairs = [(i, j) for i in range(nq) for j in range(i + 1)]
    qi_tab = jnp.asarray([i for i, _ in pairs], dtype=jnp.int32)
    ki_tab = jnp.asarray([j for _, j in pairs], dtype=jnp.int32)
    grid_spec = pltpu.PrefetchScalarGridSpec(
        num_scalar_prefetch=2,
        grid=(batch, len(pairs)),
        in_specs=[
            pl.BlockSpec((None, tq, wa), lambda b, p, qi, ki: (b, qi[p], 0)),
            pl.BlockSpec((None, tq, wa), lambda b, p, qi, ki: (b, ki[p], 0)),
            pl.BlockSpec((w, tq), lambda b, p, qi, ki: (0, b * nq + ki[p])),
        ],
        out_specs=pl.BlockSpec((None, tq, w), lambda b, p, qi, ki: (b, qi[p], 0)),
        scratch_shapes=[pltpu.VMEM((N_HEADS, tq), F32), pltpu.VMEM((N_HEADS, tq), F32), pltpu.VMEM((w, tq), F32)],
    )
    return pl.pallas_call(
        _flash_kernel,
        out_shape=jax.ShapeDtypeStruct((batch, t, w), BF16),
        grid_spec=grid_spec,
        compiler_params=_params("parallel", "arbitrary"),
        name="flash_prompt",
    )(qi_tab, ki_tab, qaug, kaug, vt)


def _paged_attn_kernel(pt_ref, q_ref, qt_ref, kn_ref, vrow_ref, lfn_ref, tri_ref, *refs, n_pages):
    k_refs = refs[:n_pages]
    v_refs = refs[n_pages:2 * n_pages]
    lf_refs = refs[2 * n_pages:3 * n_pages]
    o_ref = refs[3 * n_pages]
    acc = refs[3 * n_pages + 1]
    page = lf_refs[0].shape[-1]
    scale = HEAD_DIM ** -0.5

    lf = jnp.concatenate([r[...] for r in lf_refs], axis=0)
    tri = tri_ref[...]
    hi, mid, lo = _split3(lf)
    w = (jnp.dot(hi, tri, preferred_element_type=F32) + jnp.dot(mid, tri, preferred_element_type=F32)
         + jnp.dot(lo, tri, preferred_element_type=F32))
    tot = jnp.broadcast_to(w[:, page - 1:page], w.shape)
    row = lax.broadcasted_iota(jnp.int32, w.shape, 0)
    off = tot
    s = N_HEADS
    while s < n_pages * N_HEADS:
        off = off + jnp.where(row >= s, pltpu.roll(off, s, axis=0), 0.0)
        s *= 2
    c = w + (off - tot)

    s_new = jnp.sum(q_ref[...] * kn_ref[...], axis=-1, keepdims=True) * scale
    lane = lax.broadcasted_iota(jnp.int32, (1, W_MIX), 1)
    l_row = jnp.ones((1, W_MIX), F32)
    pn_row = jnp.zeros((1, W_MIX), F32)
    for h in range(N_HEADS):
        qcol = jnp.broadcast_to(qt_ref[:, h:h + 1] * scale, (HEAD_DIM, page))
        zs = []
        for j in range(n_pages):
            sj = jnp.sum(k_refs[j][h] * qcol, axis=0, keepdims=True)
            zs.append(sj - c[j * N_HEADS + h:j * N_HEADS + h + 1, :])
        last = (n_pages - 1) * N_HEADS + h
        zn = s_new[h:h + 1, :] - (off[last:last + 1, 0:1] + lfn_ref[h:h + 1, 0:1])
        mz = zs[0]
        for z in zs[1:]:
            mz = jnp.maximum(mz, z)
        m = jnp.maximum(jnp.max(mz, axis=-1, keepdims=True), zn)
        pn = jnp.exp(zn - m)
        lsum = jnp.zeros((1, page), F32)
        a = jnp.zeros((HEAD_DIM, page), F32)
        for j in range(n_pages):
            p = jnp.exp(zs[j] - m)
            lsum = lsum + p
            a = a + v_refs[j][h] * p
        acc[h * HEAD_DIM:(h + 1) * HEAD_DIM, :] = a
        l = jnp.sum(lsum, axis=-1, keepdims=True) + pn
        mine = (lane >= h * HEAD_DIM) & (lane < (h + 1) * HEAD_DIM)
        l_row = jnp.where(mine, l, l_row)
        pn_row = jnp.where(mine, pn, pn_row)
    ones = jnp.ones((SUBLANES, page), BF16)
    nt = (((1,), (1,)), ((), ()))
    ahi, amid, alo = _split3(acc[...])
    o_pages = (lax.dot_general(ones, ahi, nt, preferred_element_type=F32)
               + lax.dot_general(ones, amid, nt, preferred_element_type=F32)
               + lax.dot_general(ones, alo, nt, preferred_element_type=F32))
    o_ref[...] = (o_pages[0:1, :] + pn_row * vrow_ref[...]) / l_row


def _paged_attn(page_table, q3, qt3, kn3, vrow, lfn_rep, cache_kt, cache_vt, cache_lft, layer):
    n = q3.shape[0]
    n_pages = page_table.shape[1]
    page = cache_kt.shape[-1]
    tri = (jnp.arange(page)[:, None] <= jnp.arange(page)[None, :]).astype(BF16)

    def per_seq(shape):
        return pl.BlockSpec((None,) + shape, lambda b, pt: (b, 0, 0))

    def kv(j):
        return pl.BlockSpec((None, None, N_HEADS, HEAD_DIM, page), lambda b, pt, j=j: (layer, pt[b * n_pages + j], 0, 0, 0))

    def lfp(j):
        return pl.BlockSpec((None, None, N_HEADS, page), lambda b, pt, j=j: (layer, pt[b * n_pages + j], 0, 0))

    grid_spec = pltpu.PrefetchScalarGridSpec(
        num_scalar_prefetch=1,
        grid=(n,),
        in_specs=[per_seq((N_HEADS, HEAD_DIM)), per_seq((HEAD_DIM, N_HEADS)), per_seq((N_HEADS, HEAD_DIM)),
                  per_seq((1, W_MIX)), per_seq((N_HEADS, LANES)), pl.BlockSpec((page, page), lambda b, pt: (0, 0))]
        + [kv(j) for j in range(n_pages)] + [kv(j) for j in range(n_pages)] + [lfp(j) for j in range(n_pages)],
        out_specs=per_seq((1, W_MIX)),
        scratch_shapes=[pltpu.VMEM((W_MIX, page), F32)],
    )
    return pl.pallas_call(
        functools.partial(_paged_attn_kernel, n_pages=n_pages),
        out_shape=jax.ShapeDtypeStruct((n, 1, W_MIX), F32),
        grid_spec=grid_spec,
        compiler_params=_params("arbitrary"),
        name="paged_attn",
    )(page_table.reshape(-1), q3, qt3, kn3, vrow, lfn_rep, tri,
      *([cache_kt] * n_pages), *([cache_vt] * n_pages), *([cache_lft] * n_pages))


def _merge_kernel(x_ref, ya_ref, yb_ref, yc_ref, g0_ref, g1_ref, g2_ref, wb_ref, wo_ref, o_ref):
    w = W_MIX
    merged = (jax.nn.sigmoid(g0_ref[...]) * jnp.dot(ya_ref[...], wb_ref[0:w, :], preferred_element_type=F32)
              + jax.nn.sigmoid(g1_ref[...]) * jnp.dot(yb_ref[...], wb_ref[w:2 * w, :], preferred_element_type=F32)
              + jax.nn.sigmoid(g2_ref[...]) * jnp.dot(yc_ref[...], wb_ref[2 * w:3 * w, :], preferred_element_type=F32))
    o_ref[...] = x_ref[...] + jnp.dot(merged.astype(BF16), wo_ref[...], preferred_element_type=F32)


def _merge(x, ya, yb, yc, proj, wb, wo, tm):
    n, d = x.shape
    w = W_MIX

    def rows(width, c=0):
        return pl.BlockSpec((tm, width), lambda i, c=c: (i, c))

    return pl.pallas_call(
        _merge_kernel,
        out_shape=jax.ShapeDtypeStruct((n, d), F32),
        grid=(n // tm,),
        in_specs=[rows(d), rows(w), rows(w), rows(w), rows(d, COL_GATES), rows(d, COL_GATES + 1), rows(d, COL_GATES + 2),
                  pl.BlockSpec((3 * w, d), lambda i: (0, 0)), pl.BlockSpec((d, d), lambda i: (0, 0))],
        out_specs=rows(d),
        compiler_params=_params("parallel"),
        name="merge",
    )(x, ya, yb, yc, proj, proj, proj, wb, wo)


def _ffn_kernel(*refs, chunk, hist, final, tiles_per_seq):
    if hist:
        x_ref, g_ref, wup_ref, cw_ref, cb_ref, wdn_ref, h0_ref, h1_ref = refs[:8]
        refs = refs[8:]
    else:
        x_ref, g_ref, wup_ref, cw_ref, cb_ref, wdn_ref = refs[:6]
        refs = refs[6:]
    if final:
        gf_ref = refs[0]
        refs = refs[1:]
    if hist:
        o_ref, st_ref = refs
    else:
        o_ref, st_ref, tail = refs

        @pl.when(pl.program_id(0) % tiles_per_seq == 0)
        def _():
            tail[...] = jnp.zeros_like(tail)

    x = x_ref[...]
    tm = x.shape[0]
    dff = wdn_ref.shape[0]
    hb = _rmsnorm(x, g_ref[...]).astype(BF16)
    acc = jnp.zeros(x.shape, F32)
    for c in range(dff // chunk):
        halves = []
        for off in (c * chunk, dff + c * chunk):
            sl = slice(off, off + chunk)
            up = jnp.dot(hb, wup_ref[:, sl], preferred_element_type=F32)
            if hist:
                prev2, prev1 = h0_ref[:, sl], h1_ref[:, sl]
                st_ref[:, sl] = up
            else:
                tl = tail[:, sl]
                prev2, prev1 = _shift_rows(up, 2, tl), _shift_rows(up, 1, tl)
                tail[:, sl] = up[tm - SUBLANES:, :]
                st_ref[:, sl] = up[tm - SUBLANES:, :]
            halves.append(cw_ref[0:1, sl] * prev2 + cw_ref[1:2, sl] * prev1 + cw_ref[2:3, sl] * up + cb_ref[:, sl])
        act = jax.nn.gelu(halves[0]) * halves[1]
        acc = acc + jnp.dot(act.astype(BF16), wdn_ref[c * chunk:(c + 1) * chunk, :], preferred_element_type=F32)
    y = x + acc
    if final:
        y = _rmsnorm(y, gf_ref[...])
    o_ref[...] = y


def _ffn(x, g, wup, cw, cb, wdn, tm, chunk, hist=None, final_g=None, tiles_per_seq=1):
    n, d = x.shape
    dup = wup.shape[1]
    nt = n // tm
    resident = dict(pipeline_mode=pl.Buffered(1))
    in_specs = [
        pl.BlockSpec((tm, d), lambda i: (i, 0)),
        pl.BlockSpec((1, d), lambda i: (0, 0)),
        pl.BlockSpec((d, dup), lambda i: (0, 0), **resident),
        pl.BlockSpec((3, dup), lambda i: (0, 0)),
        pl.BlockSpec((1, dup), lambda i: (0, 0)),
        pl.BlockSpec((dup // 2, d), lambda i: (0, 0), **resident),
    ]
    args = [x, g, wup, cw, cb, wdn]
    scratch = []
    if hist is not None:
        in_specs += [pl.BlockSpec((tm, dup), lambda i: (i, 0))] * 2
        args += list(hist)
        st_shape = jax.ShapeDtypeStruct((n, dup), F32)
        st_spec = pl.BlockSpec((tm, dup), lambda i: (i, 0))
    else:
        nseq = nt // tiles_per_seq
        st_shape = jax.ShapeDtypeStruct((nseq, SUBLANES, dup), F32)
        st_spec = pl.BlockSpec((None, SUBLANES, dup), lambda i: (i // tiles_per_seq, 0, 0))
        scratch = [pltpu.VMEM((SUBLANES, dup), F32)]
    if final_g is not None:
        in_specs.append(pl.BlockSpec((1, d), lambda i: (0, 0)))
        args.append(final_g)
    return pl.pallas_call(
        functools.partial(_ffn_kernel, chunk=chunk, hist=hist is not None, final=final_g is not None,
                          tiles_per_seq=tiles_per_seq),
        out_shape=(jax.ShapeDtypeStruct((n, d), F32), st_shape),
        grid=(nt,),
        in_specs=in_specs,
        out_specs=(pl.BlockSpec((tm, d), lambda i: (i, 0)), st_spec),
        scratch_shapes=scratch,
        compiler_params=_params("arbitrary"),
        name="ffn",
    )(*args)


def _tile(n, want):
    if n <= want:
        return n
    t = want
    while n % t:
        t -= SUBLANES
    return t


def kernel(x_prompt, x_sample, cache_k, cache_v, cache_logf, page_table, state_conv_a, state_rg_conv, state_rg_h, state_ffn_conv, norm1_g, w_in, conv_a_w, b_forget, rg_conv_w, rg_conv_b, rg_w_r, rg_b_r, rg_w_i, rg_b_i, rg_lambda, w_branch, w_out, norm2_g, ffn_w_up, ffn_conv_w, ffn_conv_b, ffn_w_down, final_norm_g):
    batch, seq, d = x_prompt.shape
    nd = x_sample.shape[0]
    depth = w_in.shape[0]
    w = W_MIX
    n_phys, page = cache_k.shape[1], cache_k.shape[2]
    n_prompt = batch * seq
    f_lo = 6 * w

    w_main = jnp.concatenate([w_in[:, :, :f_lo], w_in[:, :, f_lo + N_HEADS:]], axis=-1).astype(BF16)
    w_f = jnp.pad(w_in[:, :, f_lo:f_lo + N_HEADS], ((0, 0), (0, 0), (0, LANES - N_HEADS))).astype(BF16)
    w_vt = jnp.swapaxes(w_in[:, :, COL_V * w:(COL_V + 1) * w], 1, 2).astype(BF16)
    b_f = jnp.pad(b_forget, ((0, 0), (0, LANES - N_HEADS)))[:, None, :]
    eye = jnp.eye(rg_w_r.shape[1], dtype=F32)
    wr_bd = jnp.einsum("lnde,nm->lndme", rg_w_r, eye).reshape(depth, w, w).astype(BF16)
    wi_bd = jnp.einsum("lnde,nm->lndme", rg_w_i, eye).reshape(depth, w, w).astype(BF16)
    wb = w_branch.astype(BF16)
    wo = w_out.astype(BF16)
    wup = ffn_w_up.astype(BF16)
    wdn = ffn_w_down.astype(BF16)
    row = lambda a: a[:, None, :]
    g1, g2, rcb, rbr, rbi, lam, fcb = map(row, (norm1_g, norm2_g, rg_conv_b, rg_b_r, rg_b_i, rg_lambda, ffn_conv_b))
    gf = final_norm_g[None, :]

    cache_kt = jnp.transpose(cache_k, (0, 1, 3, 4, 2))
    cache_vt = jnp.transpose(cache_v, (0, 1, 3, 4, 2))
    cache_lft = jnp.transpose(cache_logf, (0, 1, 3, 2))

    tm_p = _tile(n_prompt, 1024)
    tt = _tile(seq, 256)
    tq = _tile(seq, 512)
    tf = _tile(seq, 512)
    xp = x_prompt.reshape(n_prompt, d)
    xs = x_sample.reshape(nd, d)
    sp, ss = [], []
    for l in range(depth):
        last = l == depth - 1
        proj, f, vt = _inproj(xp, g1[l], w_main[l], w_f[l], tm_p, 1024, wvt=w_vt[l])
        proj3 = proj.reshape(batch, seq, proj.shape[1])
        logf, qaug, kaug = _attn_prep(f, b_f[l], proj3, _tile(seq, 1024))
        ya, yc, su, sx, sh = _mixers_prompt(proj3, conv_a_w[l], rg_conv_w[l], rcb[l], wr_bd[l], wi_bd[l],
                                            rbr[l], rbi[l], lam[l], tt)
        yb = _flash(qaug, kaug, vt, tq)
        x1 = _merge(xp, ya.reshape(n_prompt, w), yb.reshape(n_prompt, w), yc.reshape(n_prompt, w), proj, wb[l], wo[l],
                    _tile(n_prompt, 512))
        xp, sf = _ffn(x1, g2[l], wup[l], ffn_conv_w[l], fcb[l], wdn[l], tf, 1536,
                      final_g=gf if last else None, tiles_per_seq=seq // tf)
        sp.append((su[:, SUBLANES - 2:], proj3[:, :, COL_K * w:(COL_K + 1) * w].reshape(batch, seq, N_HEADS, HEAD_DIM),
                   proj3[:, :, COL_V * w:(COL_V + 1) * w].reshape(batch, seq, N_HEADS, HEAD_DIM),
                   logf[:, :N_HEADS].reshape(batch, seq, N_HEADS), sx[:, SUBLANES - 3:], sh[:, 0], sf[:, SUBLANES - 2:]))
        projs, fs = _inproj(xs, g1[l], w_main[l], w_f[l], nd, 1024)
        ha, hc, hf = state_conv_a[l], state_rg_conv[l], state_ffn_conv[l]
        yas, ycs, us, hs, lfs = _mixers_sample(projs, ha[:, 0], ha[:, 1], hc[:, 0], hc[:, 1], hc[:, 2], state_rg_h[l],
                                               conv_a_w[l], rg_conv_w[l], rcb[l], wr_bd[l], wi_bd[l], rbr[l], rbi[l],
                                               lam[l], fs, b_f[l])
        q3 = projs[:, COL_Q * w:(COL_Q + 1) * w].reshape(nd, N_HEADS, HEAD_DIM)
        k3 = projs[:, COL_K * w:(COL_K + 1) * w].reshape(nd, N_HEADS, HEAD_DIM)
        v3 = projs[:, COL_V * w:(COL_V + 1) * w].reshape(nd, N_HEADS, HEAD_DIM)
        ybs = _paged_attn(page_table, q3, jnp.swapaxes(q3, 1, 2), k3, projs[:, None, COL_V * w:(COL_V + 1) * w],
                          jnp.broadcast_to(lfs[:, :N_HEADS, None], (nd, N_HEADS, LANES)),
                          cache_kt, cache_vt, cache_lft, l)
        x1s = _merge(xs, yas, ybs.reshape(nd, w).astype(BF16), ycs, projs, wb[l], wo[l], nd)
        xs, ups = _ffn(x1s, g2[l], wup[l], ffn_conv_w[l], fcb[l], wdn[l], nd, 512, hist=(hf[:, 0], hf[:, 1]),
                       final_g=gf if last else None)
        ss.append((jnp.stack([ha[:, 1], us], axis=1), k3[:, None], v3[:, None], lfs[:, None, :N_HEADS],
                   jnp.stack([hc[:, 1], hc[:, 2], projs[:, COL_XC * w:(COL_XC + 1) * w]], axis=1), hs,
                   jnp.stack([hf[:, 1], ups], axis=1)))

    def stk(outs, i):
        return jnp.stack([o[i] for o in outs], axis=0)

    return (xp.reshape(batch, seq, d), xs.reshape(nd, 1, d),
            stk(sp, 0), stk(sp, 1), stk(sp, 2), stk(sp, 3), stk(sp, 4), stk(sp, 5), stk(sp, 6),
            stk(ss, 0), stk(ss, 1), stk(ss, 2), stk(ss, 3), stk(ss, 4), stk(ss, 5), stk(ss, 6))
```

```python
import functools

import jax
import jax.numpy as jnp
from jax import lax
from jax.experimental import pallas as pl
from jax.experimental.pallas import tpu as pltpu

F32 = jnp.float32
BF16 = jnp.bfloat16

EPS = 1e-6
LRU_C = 8.0
N_HEADS = 8
HEAD_DIM = 64
W_MIX = N_HEADS * HEAD_DIM
NEG = -1e30
LOG2E = 1.4426950408889634
FLASH_LOOKAHEAD = 2
NORM_SLACK = 1.02
SKIP_LOG2 = 158.0
LANES = 128
SUBLANES = 8
VMEM_LIMIT = 56 * 1024 * 1024

COL_BA, COL_CA, COL_XA, COL_Q, COL_K, COL_V, COL_XC, COL_GC = range(8)
COL_GATES = 4


def _params(*sem):
    return pltpu.CompilerParams(dimension_semantics=sem, vmem_limit_bytes=VMEM_LIMIT)


def _rmsnorm(x, g):
    ms = jnp.mean(x * x, axis=-1, keepdims=True)
    return x * lax.rsqrt(ms + EPS) * g


def _softplus(y):
    return jnp.maximum(y, 0.0) + jnp.log1p(jnp.exp(-jnp.abs(y)))


def _log_sigmoid(z):
    return -_softplus(-z)


def _bdot(a, b):
    return jnp.dot(a.astype(BF16), b.astype(BF16), preferred_element_type=F32)


def _split3(x):
    hi = x.astype(BF16)
    r1 = x - hi.astype(F32)
    mid = r1.astype(BF16)
    lo = (r1 - mid.astype(F32)).astype(BF16)
    return hi, mid, lo


def _shift_rows(x, k, tail):
    y = pltpu.roll(x, k, axis=0)
    row = lax.broadcasted_iota(jnp.int32, tail.shape, 0)
    first = jnp.where(row < k, pltpu.roll(tail, k, axis=0), y[:SUBLANES])
    return jnp.concatenate([first, y[SUBLANES:]], axis=0)


def _inproj_kernel(*refs, with_vt):
    if with_vt:
        x_ref, g_ref, w_ref, wf_ref, wvt_ref, proj_ref, f_ref, vt_ref, h_scr = refs
    else:
        x_ref, g_ref, w_ref, wf_ref, proj_ref, f_ref, h_scr = refs

    @pl.when(pl.program_id(1) == 0)
    def _():
        hb = _rmsnorm(x_ref[...], g_ref[...]).astype(BF16)
        h_scr[...] = hb
        f_ref[...] = jnp.dot(hb, wf_ref[...], preferred_element_type=F32)
        if with_vt:
            vt = lax.dot_general(wvt_ref[...], hb, (((1,), (1,)), ((), ())), preferred_element_type=F32)
            vt_ref[...] = vt.astype(BF16)

    proj_ref[...] = jnp.dot(h_scr[...], w_ref[...], preferred_element_type=F32)


def _inproj(x, g, w, wf, tm, tn, wvt=None):
    n, d = x.shape
    nw = w.shape[1]
    in_specs = [
        pl.BlockSpec((tm, d), lambda i, j: (i, 0)),
        pl.BlockSpec((1, d), lambda i, j: (0, 0)),
        pl.BlockSpec((d, tn), lambda i, j: (0, j)),
        pl.BlockSpec((d, LANES), lambda i, j: (0, 0)),
    ]
    out_shape = [jax.ShapeDtypeStruct((n, nw), F32), jax.ShapeDtypeStruct((n, LANES), F32)]
    out_specs = [pl.BlockSpec((tm, tn), lambda i, j: (i, j)), pl.BlockSpec((tm, LANES), lambda i, j: (i, 0))]
    args = [x, g, w, wf]
    if wvt is not None:
        in_specs.append(pl.BlockSpec((W_MIX, d), lambda i, j: (0, 0)))
        out_shape.append(jax.ShapeDtypeStruct((W_MIX, n), BF16))
        out_specs.append(pl.BlockSpec((W_MIX, tm), lambda i, j: (0, i)))
        args.append(wvt)
    return pl.pallas_call(
        functools.partial(_inproj_kernel, with_vt=wvt is not None),
        out_shape=tuple(out_shape),
        grid=(n // tm, nw // tn),
        in_specs=in_specs,
        out_specs=tuple(out_specs),
        scratch_shapes=[pltpu.VMEM((tm, d), BF16)],
        compiler_params=_params("parallel", "arbitrary"),
        name="inproj",
    )(*args)


def _attn_prep_kernel(f_ref, b_ref, q_ref, k_ref, place_ref, ones_ref, e_ref, hsum_ref,
                      logf_ref, qa_ref, ka_ref, st_ref, carry, *, tile):
    @pl.when(pl.program_id(1) == 0)
    def _():
        carry[...] = jnp.zeros_like(carry)

    lf = _log_sigmoid(f_ref[...] + b_ref[...])
    logf_ref[...] = lf
    tc = lf.shape[0]
    row = lax.broadcasted_iota(jnp.int32, lf.shape, 0)
    c = lf
    s = 1
    while s < tc:
        c = c + jnp.where(row >= s, pltpu.roll(c, s, axis=0), 0.0)
        s *= 2
    c = c + carry[...]
    carry[...] = c[tc - 1:tc, :]
    c2 = c * LOG2E
    c_hi, c_mid, c_lo = _split3(c2)
    qb = (q_ref[...] * (LOG2E * HEAD_DIM ** -0.5)).astype(BF16)
    kb = k_ref[...].astype(BF16)
    qa_ref[...] = (jnp.dot(qb, place_ref[...], preferred_element_type=F32) + ones_ref[...]).astype(BF16)
    ka = (jnp.dot(kb, place_ref[...], preferred_element_type=F32)
          + jnp.dot(c_hi, e_ref[0], preferred_element_type=F32)
          + jnp.dot(c_mid, e_ref[1], preferred_element_type=F32)
          + jnp.dot(c_lo, e_ref[2], preferred_element_type=F32))
    ka_ref[...] = ka.astype(BF16)

    def sqnorm(xb):
        xf = xb.astype(F32)
        return jnp.dot((xf * xf).astype(BF16), hsum_ref[...], preferred_element_type=F32)

    kn2, qn2 = sqnorm(kb), sqnorm(qb)
    st_ref[...] = jnp.zeros_like(st_ref)
    for t in range(tc // tile):
        rows = slice(t * tile, (t + 1) * tile)
        st_ref[t, 0:1, :] = jnp.sqrt(jnp.max(kn2[rows, :], axis=0, keepdims=True) * NORM_SLACK)
        st_ref[t, 1:2, :] = jnp.sqrt(jnp.max(qn2[rows, :], axis=0, keepdims=True) * NORM_SLACK)
        st_ref[t, 2:3, :] = jnp.max(-c2[rows, :], axis=0, keepdims=True)


def _attn_prep(f, b_pad, proj3, tc, tile):
    batch, t, _ = proj3.shape
    n = batch * t
    nt = t // tc
    w = W_MIX
    wa = N_HEADS * LANES
    head_sum = (jnp.arange(w, dtype=jnp.int32)[:, None] // HEAD_DIM == jnp.arange(LANES, dtype=jnp.int32)[None, :]).astype(BF16)
    src = jnp.arange(w, dtype=jnp.int32)
    dst = jnp.arange(wa, dtype=jnp.int32)
    place = (dst[None, :] == (src[:, None] // HEAD_DIM) * LANES + src[:, None] % HEAD_DIM).astype(BF16)
    lane = dst % LANES
    ones_row = ((lane >= HEAD_DIM) & (lane < HEAD_DIM + 3)).astype(F32)[None, :]
    hsrc = jnp.arange(LANES, dtype=jnp.int32)
    e = jnp.stack([-((dst[None, :] == hsrc[:, None] * LANES + HEAD_DIM + part) & (hsrc[:, None] < N_HEADS)).astype(F32)
                   for part in range(3)]).astype(BF16)
    per = tc // tile
    return pl.pallas_call(
        functools.partial(_attn_prep_kernel, tile=tile),
        out_shape=(jax.ShapeDtypeStruct((n, LANES), F32), jax.ShapeDtypeStruct((batch, t, wa), BF16),
                   jax.ShapeDtypeStruct((batch, t, wa), BF16),
                   jax.ShapeDtypeStruct((batch, t // tile, SUBLANES, LANES), F32)),
        grid=(batch, nt),
        in_specs=[
            pl.BlockSpec((tc, LANES), lambda b, i: (b * nt + i, 0)),
            pl.BlockSpec((1, LANES), lambda b, i: (0, 0)),
            pl.BlockSpec((None, tc, w), lambda b, i: (b, i, COL_Q)),
            pl.BlockSpec((None, tc, w), lambda b, i: (b, i, COL_K)),
            pl.BlockSpec((w, wa), lambda b, i: (0, 0)),
            pl.BlockSpec((1, wa), lambda b, i: (0, 0)),
            pl.BlockSpec((3, LANES, wa), lambda b, i: (0, 0, 0)),
            pl.BlockSpec((w, LANES), lambda b, i: (0, 0)),
        ],
        out_specs=(
            pl.BlockSpec((tc, LANES), lambda b, i: (b * nt + i, 0)),
            pl.BlockSpec((None, tc, wa), lambda b, i: (b, i, 0)),
            pl.BlockSpec((None, tc, wa), lambda b, i: (b, i, 0)),
            pl.BlockSpec((None, per, SUBLANES, LANES), lambda b, i: (b, i, 0, 0)),
        ),
        scratch_shapes=[pltpu.VMEM((1, LANES), F32)],
        compiler_params=_params("parallel", "arbitrary"),
        name="attn_prep",
    )(f, b_pad, proj3, proj3, place, ones_row, e, head_sum)


def _rglru_terms(xc, wr_ref, wi_ref, br_ref, bi_ref, lam_ref):
    r = jax.nn.sigmoid(_bdot(xc, wr_ref[...]) + br_ref[...])
    i = jax.nn.sigmoid(_bdot(xc, wi_ref[...]) + bi_ref[...])
    log_a = -LRU_C * r * _softplus(-lam_ref[...])
    a = jnp.exp(log_a)
    bterm = jnp.sqrt(-jnp.tanh(log_a) * (1.0 + a * a)) * i * xc
    return a, bterm


def _mixers_prompt_kernel(ba_ref, ca_ref, xa_ref, xc_ref, gc_ref, caw_ref, rcw_ref, rcb_ref,
                          wr_ref, wi_ref, br_ref, bi_ref, lam_ref,
                          ya_ref, yc_ref, su_ref, sx_ref, sh_ref, u_tail, x_tail, h_carry):
    @pl.when(pl.program_id(1) == 0)
    def _():
        u_tail[...] = jnp.zeros_like(u_tail)
        x_tail[...] = jnp.zeros_like(x_tail)
        h_carry[...] = jnp.zeros_like(h_carry)

    tt = ba_ref.shape[0]
    u = ca_ref[...] * xa_ref[...]
    ut = u_tail[...]
    ua = caw_ref[0:1, :] * _shift_rows(u, 2, ut) + caw_ref[1:2, :] * _shift_rows(u, 1, ut) + caw_ref[2:3, :] * u
    ya_ref[...] = (ba_ref[...] * ua).astype(BF16)
    u_tail[...] = u[tt - SUBLANES:, :]
    su_ref[...] = u[tt - SUBLANES:, :]
    x = xc_ref[...]
    xt = x_tail[...]
    xc = (rcw_ref[0:1, :] * _shift_rows(x, 3, xt) + rcw_ref[1:2, :] * _shift_rows(x, 2, xt)
          + rcw_ref[2:3, :] * _shift_rows(x, 1, xt) + rcw_ref[3:4, :] * x + rcb_ref[...])
    x_tail[...] = x[tt - SUBLANES:, :]
    sx_ref[...] = x[tt - SUBLANES:, :]
    a, b = _rglru_terms(xc, wr_ref, wi_ref, br_ref, bi_ref, lam_ref)
    row = lax.broadcasted_iota(jnp.int32, a.shape, 0)
    s = 1
    while s < tt:
        keep = row >= s
        a_prev = jnp.where(keep, pltpu.roll(a, s, axis=0), 1.0)
        b_prev = jnp.where(keep, pltpu.roll(b, s, axis=0), 0.0)
        b = a * b_prev + b
        a = a * a_prev
        s *= 2
    h = a * h_carry[...] + b
    h_carry[...] = h[tt - 1:tt, :]
    sh_ref[...] = h[tt - 1:tt, :]
    yc_ref[...] = (h * jax.nn.gelu(gc_ref[...])).astype(BF16)


def _mixers_prompt(proj3, caw, rcw, rcb, wr, wi, br, bi, lam, tt):
    batch, t, _ = proj3.shape
    w = W_MIX
    nt = t // tt

    def col(c):
        return pl.BlockSpec((None, tt, w), lambda b, i, c=c: (b, i, c))

    def const(shape):
        return pl.BlockSpec(shape, lambda b, i: (0,) * len(shape))

    return pl.pallas_call(
        _mixers_prompt_kernel,
        out_shape=(
            jax.ShapeDtypeStruct((batch, t, w), BF16),
            jax.ShapeDtypeStruct((batch, t, w), BF16),
            jax.ShapeDtypeStruct((batch, SUBLANES, w), F32),
            jax.ShapeDtypeStruct((batch, SUBLANES, w), F32),
            jax.ShapeDtypeStruct((batch, 1, w), F32),
        ),
        grid=(batch, nt),
        in_specs=[col(COL_BA), col(COL_CA), col(COL_XA), col(COL_XC), col(COL_GC),
                  const((3, w)), const((4, w)), const((1, w)), const((w, w)), const((w, w)),
                  const((1, w)), const((1, w)), const((1, w))],
        out_specs=(
            pl.BlockSpec((None, tt, w), lambda b, i: (b, i, 0)),
            pl.BlockSpec((None, tt, w), lambda b, i: (b, i, 0)),
            pl.BlockSpec((None, SUBLANES, w), lambda b, i: (b, 0, 0)),
            pl.BlockSpec((None, SUBLANES, w), lambda b, i: (b, 0, 0)),
            pl.BlockSpec((None, 1, w), lambda b, i: (b, 0, 0)),
        ),
        scratch_shapes=[pltpu.VMEM((SUBLANES, w), F32), pltpu.VMEM((SUBLANES, w), F32), pltpu.VMEM((1, w), F32)],
        compiler_params=_params("parallel", "arbitrary"),
        name="mixers_prompt",
    )(proj3, proj3, proj3, proj3, proj3, caw, rcw, rcb, wr, wi, br, bi, lam)


def _mixers_sample_kernel(ba_ref, ca_ref, xa_ref, xc_ref, gc_ref, ha0_ref, ha1_ref, hc0_ref, hc1_ref, hc2_ref, h0_ref,
                          caw_ref, rcw_ref, rcb_ref, wr_ref, wi_ref, br_ref, bi_ref, lam_ref, f_ref, bf_ref,
                          ya_ref, yc_ref, u_ref, h_ref, lf_ref):
    lf_ref[...] = _log_sigmoid(f_ref[...] + bf_ref[...])
    u = ca_ref[...] * xa_ref[...]
    ua = caw_ref[0:1, :] * ha0_ref[...] + caw_ref[1:2, :] * ha1_ref[...] + caw_ref[2:3, :] * u
    ya_ref[...] = (ba_ref[...] * ua).astype(BF16)
    u_ref[...] = u
    xc = (rcw_ref[0:1, :] * hc0_ref[...] + rcw_ref[1:2, :] * hc1_ref[...] + rcw_ref[2:3, :] * hc2_ref[...]
          + rcw_ref[3:4, :] * xc_ref[...] + rcb_ref[...])
    a, b = _rglru_terms(xc, wr_ref, wi_ref, br_ref, bi_ref, lam_ref)
    h = a * h0_ref[...] + b
    h_ref[...] = h
    yc_ref[...] = (h * jax.nn.gelu(gc_ref[...])).astype(BF16)


def _mixers_sample(proj, ha0, ha1, hc0, hc1, hc2, h0, caw, rcw, rcb, wr, wi, br, bi, lam, f, bf):
    n = proj.shape[0]
    w = W_MIX

    def col(c):
        return pl.BlockSpec((n, w), lambda i, c=c: (0, c))

    def full(shape):
        return pl.BlockSpec(shape, lambda i: (0,) * len(shape))

    return pl.pallas_call(
        _mixers_sample_kernel,
        out_shape=(jax.ShapeDtypeStruct((n, w), BF16), jax.ShapeDtypeStruct((n, w), BF16),
                   jax.ShapeDtypeStruct((n, w), F32), jax.ShapeDtypeStruct((n, w), F32),
                   jax.ShapeDtypeStruct((n, LANES), F32)),
        grid=(1,),
        in_specs=[col(COL_BA), col(COL_CA), col(COL_XA), col(COL_XC), col(COL_GC)]
        + [full((n, w))] * 6
        + [full((3, w)), full((4, w)), full((1, w)), full((w, w)), full((w, w)), full((1, w)), full((1, w)), full((1, w)),
           full((n, LANES)), full((1, LANES))],
        out_specs=(full((n, w)), full((n, w)), full((n, w)), full((n, w)), full((n, LANES))),
        compiler_params=_params("arbitrary"),
        name="mixers_sample",
    )(proj, proj, proj, proj, proj, ha0, ha1, hc0, hc1, hc2, h0, caw, rcw, rcb, wr, wi, br, bi, lam, f, bf)


def _flash_kernel(qi_ref, ki_ref, qn_ref, kn_ref, cn_ref, qa_ref, ka_ref, vt_ref, o_ref, m_s, l_s, acc, *, nq):
    b = pl.program_id(0)
    p = pl.program_id(1)
    qi = qi_ref[p]
    ki = ki_ref[p]
    tq = qa_ref.shape[0]
    tk = ka_ref.shape[0]

    @pl.when(ki == qi)
    def _():
        m_s[...] = jnp.full_like(m_s, NEG)
        l_s[...] = jnp.zeros_like(l_s)
        acc[...] = jnp.zeros_like(acc)

    def step(diagonal):
        if diagonal:
            krow = lax.broadcasted_iota(jnp.int32, (tk, tq), 0)
            qcol = lax.broadcasted_iota(jnp.int32, (tk, tq), 1)
            causal = krow <= qcol

        def scores(h):
            la = slice(h * LANES, (h + 1) * LANES)
            return lax.dot_general(ka_ref[:, la], qa_ref[:, la], (((1,), (1,)), ((), ())),
                                   preferred_element_type=F32)

        sts = [scores(h) for h in range(FLASH_LOOKAHEAD)]
        for h in range(N_HEADS):
            rows = slice(h * HEAD_DIM, (h + 1) * HEAD_DIM)
            if h + FLASH_LOOKAHEAD < N_HEADS:
                sts.append(scores(h + FLASH_LOOKAHEAD))
            st = sts[h]
            if diagonal:
                st = jnp.where(causal, st, NEG)
            m_prev = m_s[h:h + 1, :]
            m_new = jnp.maximum(m_prev, jnp.max(st, axis=0, keepdims=True))
            alpha = jnp.exp2(m_prev - m_new)
            pt = jnp.exp2(st - m_new)
            l_s[h:h + 1, :] = alpha * l_s[h:h + 1, :] + jnp.sum(pt, axis=0, keepdims=True)
            acc[rows, :] = alpha * acc[rows, :] + jnp.dot(vt_ref[rows, :], pt.astype(BF16),
                                                          preferred_element_type=F32)
            m_s[h:h + 1, :] = m_new

    @pl.when(ki == qi)
    def _():
        step(True)

    @pl.when(ki < qi)
    def _():
        m_min = jnp.min(m_s[...], axis=1, keepdims=True)
        gap = None
        for h in range(N_HEADS):
            iq = (b * nq + qi) * N_HEADS + h
            ik = (b * nq + ki) * N_HEADS + h
            g = qn_ref[iq] * kn_ref[ik] + cn_ref[ik] - m_min[h, 0]
            gap = g if gap is None else jnp.maximum(gap, g)

        @pl.when(gap > -SKIP_LOG2)
        def _():
            step(False)

    @pl.when(ki == 0)
    def _():
        for h in range(N_HEADS):
            rows = slice(h * HEAD_DIM, (h + 1) * HEAD_DIM)
            acc[rows, :] = acc[rows, :] / l_s[h:h + 1, :]
        o_ref[...] = acc[...].T.astype(BF16)


def _flash(qaug, kaug, vt, stats, tq):
    batch, t, wa = qaug.shape
    w = W_MIX
    nq = t // tq
    pairs = [(i, j) for i in range(nq) for j in range(i, -1, -1)]
    qi_tab = jnp.asarray([i for i, _ in pairs], dtype=jnp.int32)
    ki_tab = jnp.asarray([j for _, j in pairs], dtype=jnp.int32)
    kn, qn, cn = (stats[:, :, r, :N_HEADS].reshape(-1) for r in range(3))
    grid_spec = pltpu.PrefetchScalarGridSpec(
        num_scalar_prefetch=5,
        grid=(batch, len(pairs)),
        in_specs=[
            pl.BlockSpec((None, tq, wa), lambda b, p, qi, ki, *_: (b, qi[p], 0)),
            pl.BlockSpec((None, tq, wa), lambda b, p, qi, ki, *_: (b, ki[p], 0)),
            pl.BlockSpec((w, tq), lambda b, p, qi, ki, *_: (0, b * nq + ki[p])),
        ],
        out_specs=pl.BlockSpec((None, tq, w), lambda b, p, qi, ki, *_: (b, qi[p], 0)),
        scratch_shapes=[pltpu.VMEM((N_HEADS, tq), F32), pltpu.VMEM((N_HEADS, tq), F32), pltpu.VMEM((w, tq), F32)],
    )
    return pl.pallas_call(
        functools.partial(_flash_kernel, nq=nq),
        out_shape=jax.ShapeDtypeStruct((batch, t, w), BF16),
        grid_spec=grid_spec,
        compiler_params=_params("parallel", "arbitrary"),
        name="flash_prompt",
    )(qi_tab, ki_tab, qn, kn, cn, qaug, kaug, vt)


def _paged_attn_kernel(pt_ref, q_ref, qt_ref, kn_ref, vrow_ref, lfn_ref, tri_ref, *refs, n_pages):
    k_refs = refs[:n_pages]
    v_refs = refs[n_pages:2 * n_pages]
    lf_refs = refs[2 * n_pages:3 * n_pages]
    o_ref = refs[3 * n_pages]
    acc = refs[3 * n_pages + 1]
    page = lf_refs[0].shape[-1]
    scale = HEAD_DIM ** -0.5

    lf = jnp.concatenate([r[...] for r in lf_refs], axis=0)
    tri = tri_ref[...]
    hi, mid, lo = _split3(lf)
    w = (jnp.dot(hi, tri, preferred_element_type=F32) + jnp.dot(mid, tri, preferred_element_type=F32)
         + jnp.dot(lo, tri, preferred_element_type=F32))
    tot = jnp.broadcast_to(w[:, page - 1:page], w.shape)
    row = lax.broadcasted_iota(jnp.int32, w.shape, 0)
    off = tot
    s = N_HEADS
    while s < n_pages * N_HEADS:
        off = off + jnp.where(row >= s, pltpu.roll(off, s, axis=0), 0.0)
        s *= 2
    c = w + (off - tot)

    s_new = jnp.sum(q_ref[...] * kn_ref[...], axis=-1, keepdims=True) * scale
    lane = lax.broadcasted_iota(jnp.int32, (1, W_MIX), 1)
    l_row = jnp.ones((1, W_MIX), F32)
    pn_row = jnp.zeros((1, W_MIX), F32)
    for h in range(N_HEADS):
        qcol = jnp.broadcast_to(qt_ref[:, h:h + 1] * scale, (HEAD_DIM, page))
        zs = []
        for j in range(n_pages):
            sj = jnp.sum(k_refs[j][h] * qcol, axis=0, keepdims=True)
            zs.append(sj - c[j * N_HEADS + h:j * N_HEADS + h + 1, :])
        last = (n_pages - 1) * N_HEADS + h
        zn = s_new[h:h + 1, :] - (off[last:last + 1, 0:1] + lfn_ref[h:h + 1, 0:1])
        mz = zs[0]
        for z in zs[1:]:
            mz = jnp.maximum(mz, z)
        m = jnp.maximum(jnp.max(mz, axis=-1, keepdims=True), zn)
        pn = jnp.exp(zn - m)
        lsum = jnp.zeros((1, page), F32)
        a = jnp.zeros((HEAD_DIM, page), F32)
        for j in range(n_pages):
            p = jnp.exp(zs[j] - m)
            lsum = lsum + p
            a = a + v_refs[j][h] * p
        acc[h * HEAD_DIM:(h + 1) * HEAD_DIM, :] = a
        l = jnp.sum(lsum, axis=-1, keepdims=True) + pn
        mine = (lane >= h * HEAD_DIM) & (lane < (h + 1) * HEAD_DIM)
        l_row = jnp.where(mine, l, l_row)
        pn_row = jnp.where(mine, pn, pn_row)
    ones = jnp.ones((SUBLANES, page), BF16)
    nt = (((1,), (1,)), ((), ()))
    ahi, amid, alo = _split3(acc[...])
    o_pages = (lax.dot_general(ones, ahi, nt, preferred_element_type=F32)
               + lax.dot_general(ones, amid, nt, preferred_element_type=F32)
               + lax.dot_general(ones, alo, nt, preferred_element_type=F32))
    o_ref[...] = (o_pages[0:1, :] + pn_row * vrow_ref[...]) / l_row


def _paged_attn(page_table, q3, qt3, kn3, vrow, lfn_rep, cache_kt, cache_vt, cache_lft, layer):
    n = q3.shape[0]
    n_pages = page_table.shape[1]
    page = cache_kt.shape[-1]
    tri = (jnp.arange(page)[:, None] <= jnp.arange(page)[None, :]).astype(BF16)

    def per_seq(shape):
        return pl.BlockSpec((None,) + shape, lambda b, pt: (b, 0, 0))

    def kv(j):
        return pl.BlockSpec((None, None, N_HEADS, HEAD_DIM, page), lambda b, pt, j=j: (layer, pt[b * n_pages + j], 0, 0, 0))

    def lfp(j):
        return pl.BlockSpec((None, None, N_HEADS, page), lambda b, pt, j=j: (layer, pt[b * n_pages + j], 0, 0))

    grid_spec = pltpu.PrefetchScalarGridSpec(
        num_scalar_prefetch=1,
        grid=(n,),
        in_specs=[per_seq((N_HEADS, HEAD_DIM)), per_seq((HEAD_DIM, N_HEADS)), per_seq((N_HEADS, HEAD_DIM)),
                  per_seq((1, W_MIX)), per_seq((N_HEADS, LANES)), pl.BlockSpec((page, page), lambda b, pt: (0, 0))]
        + [kv(j) for j in range(n_pages)] + [kv(j) for j in range(n_pages)] + [lfp(j) for j in range(n_pages)],
        out_specs=per_seq((1, W_MIX)),
        scratch_shapes=[pltpu.VMEM((W_MIX, page), F32)],
    )
    return pl.pallas_call(
        functools.partial(_paged_attn_kernel, n_pages=n_pages),
        out_shape=jax.ShapeDtypeStruct((n, 1, W_MIX), F32),
        grid_spec=grid_spec,
        compiler_params=_params("arbitrary"),
        name="paged_attn",
    )(page_table.reshape(-1), q3, qt3, kn3, vrow, lfn_rep, tri,
      *([cache_kt] * n_pages), *([cache_vt] * n_pages), *([cache_lft] * n_pages))


def _merge_kernel(x_ref, ya_ref, yb_ref, yc_ref, g0_ref, g1_ref, g2_ref, wb_ref, wo_ref, o_ref):
    w = W_MIX
    merged = (jax.nn.sigmoid(g0_ref[...]) * jnp.dot(ya_ref[...], wb_ref[0:w, :], preferred_element_type=F32)
              + jax.nn.sigmoid(g1_ref[...]) * jnp.dot(yb_ref[...], wb_ref[w:2 * w, :], preferred_element_type=F32)
              + jax.nn.sigmoid(g2_ref[...]) * jnp.dot(yc_ref[...], wb_ref[2 * w:3 * w, :], preferred_element_type=F32))
    o_ref[...] = x_ref[...] + jnp.dot(merged.astype(BF16), wo_ref[...], preferred_element_type=F32)


def _merge(x, ya, yb, yc, proj, wb, wo, tm):
    n, d = x.shape
    w = W_MIX

    def rows(width, c=0):
        return pl.BlockSpec((tm, width), lambda i, c=c: (i, c))

    return pl.pallas_call(
        _merge_kernel,
        out_shape=jax.ShapeDtypeStruct((n, d), F32),
        grid=(n // tm,),
        in_specs=[rows(d), rows(w), rows(w), rows(w), rows(d, COL_GATES), rows(d, COL_GATES + 1), rows(d, COL_GATES + 2),
                  pl.BlockSpec((3 * w, d), lambda i: (0, 0)), pl.BlockSpec((d, d), lambda i: (0, 0))],
        out_specs=rows(d),
        compiler_params=_params("parallel"),
        name="merge",
    )(x, ya, yb, yc, proj, proj, proj, wb, wo)


def _ffn_kernel(*refs, chunk, hist, final, tiles_per_seq):
    if hist:
        x_ref, g_ref, wup_ref, cw_ref, cb_ref, wdn_ref, h0_ref, h1_ref = refs[:8]
        refs = refs[8:]
    else:
        x_ref, g_ref, wup_ref, cw_ref, cb_ref, wdn_ref = refs[:6]
        refs = refs[6:]
    if final:
        gf_ref = refs[0]
        refs = refs[1:]
    if hist:
        o_ref, st_ref = refs
    else:
        o_ref, st_ref, tail = refs

        @pl.when(pl.program_id(0) % tiles_per_seq == 0)
        def _():
            tail[...] = jnp.zeros_like(tail)

    x = x_ref[...]
    tm = x.shape[0]
    dff = wdn_ref.shape[0]
    hb = _rmsnorm(x, g_ref[...]).astype(BF16)
    acc = jnp.zeros(x.shape, F32)
    for c in range(dff // chunk):
        halves = []
        for off in (c * chunk, dff + c * chunk):
            sl = slice(off, off + chunk)
            up = jnp.dot(hb, wup_ref[:, sl], preferred_element_type=F32)
            if hist:
                prev2, prev1 = h0_ref[:, sl], h1_ref[:, sl]
                st_ref[:, sl] = up
            else:
                tl = tail[:, sl]
                prev2, prev1 = _shift_rows(up, 2, tl), _shift_rows(up, 1, tl)
                tail[:, sl] = up[tm - SUBLANES:, :]
                st_ref[:, sl] = up[tm - SUBLANES:, :]
            halves.append(cw_ref[0:1, sl] * prev2 + cw_ref[1:2, sl] * prev1 + cw_ref[2:3, sl] * up + cb_ref[:, sl])
        act = jax.nn.gelu(halves[0]) * halves[1]
        acc = acc + jnp.dot(act.astype(BF16), wdn_ref[c * chunk:(c + 1) * chunk, :], preferred_element_type=F32)
    y = x + acc
    if final:
        y = _rmsnorm(y, gf_ref[...])
    o_ref[...] = y


def _ffn(x, g, wup, cw, cb, wdn, tm, chunk, hist=None, final_g=None, tiles_per_seq=1):
    n, d = x.shape
    dup = wup.shape[1]
    nt = n // tm
    resident = dict(pipeline_mode=pl.Buffered(1))
    in_specs = [
        pl.BlockSpec((tm, d), lambda i: (i, 0)),
        pl.BlockSpec((1, d), lambda i: (0, 0)),
        pl.BlockSpec((d, dup), lambda i: (0, 0), **resident),
        pl.BlockSpec((3, dup), lambda i: (0, 0)),
        pl.BlockSpec((1, dup), lambda i: (0, 0)),
        pl.BlockSpec((dup // 2, d), lambda i: (0, 0), **resident),
    ]
    args = [x, g, wup, cw, cb, wdn]
    scratch = []
    if hist is not None:
        in_specs += [pl.BlockSpec((tm, dup), lambda i: (i, 0))] * 2
        args += list(hist)
        st_shape = jax.ShapeDtypeStruct((n, dup), F32)
        st_spec = pl.BlockSpec((tm, dup), lambda i: (i, 0))
    else:
        nseq = nt // tiles_per_seq
        st_shape = jax.ShapeDtypeStruct((nseq, SUBLANES, dup), F32)
        st_spec = pl.BlockSpec((None, SUBLANES, dup), lambda i: (i // tiles_per_seq, 0, 0))
        scratch = [pltpu.VMEM((SUBLANES, dup), F32)]
    if final_g is not None:
        in_specs.append(pl.BlockSpec((1, d), lambda i: (0, 0)))
        args.append(final_g)
    return pl.pallas_call(
        functools.partial(_ffn_kernel, chunk=chunk, hist=hist is not None, final=final_g is not None,
                          tiles_per_seq=tiles_per_seq),
        out_shape=(jax.ShapeDtypeStruct((n, d), F32), st_shape),
        grid=(nt,),
        in_specs=in_specs,
        out_specs=(pl.BlockSpec((tm, d), lambda i: (i, 0)), st_spec),
        scratch_shapes=scratch,
        compiler_params=_params("arbitrary"),
        name="ffn",
    )(*args)


def _tile(n, want):
    if n <= want:
        return n
    t = want
    while n % t:
        t -= SUBLANES
    return t


def kernel(x_prompt, x_sample, cache_k, cache_v, cache_logf, page_table, state_conv_a, state_rg_conv, state_rg_h, state_ffn_conv, norm1_g, w_in, conv_a_w, b_forget, rg_conv_w, rg_conv_b, rg_w_r, rg_b_r, rg_w_i, rg_b_i, rg_lambda, w_branch, w_out, norm2_g, ffn_w_up, ffn_conv_w, ffn_conv_b, ffn_w_down, final_norm_g):
    batch, seq, d = x_prompt.shape
    nd = x_sample.shape[0]
    depth = w_in.shape[0]
    w = W_MIX
    n_phys, page = cache_k.shape[1], cache_k.shape[2]
    n_prompt = batch * seq
    f_lo = 6 * w

    w_main = jnp.concatenate([w_in[:, :, :f_lo], w_in[:, :, f_lo + N_HEADS:]], axis=-1).astype(BF16)
    w_f = jnp.pad(w_in[:, :, f_lo:f_lo + N_HEADS], ((0, 0), (0, 0), (0, LANES - N_HEADS))).astype(BF16)
    w_vt = jnp.swapaxes(w_in[:, :, COL_V * w:(COL_V + 1) * w], 1, 2).astype(BF16)
    b_f = jnp.pad(b_forget, ((0, 0), (0, LANES - N_HEADS)))[:, None, :]
    eye = jnp.eye(rg_w_r.shape[1], dtype=F32)
    wr_bd = jnp.einsum("lnde,nm->lndme", rg_w_r, eye).reshape(depth, w, w).astype(BF16)
    wi_bd = jnp.einsum("lnde,nm->lndme", rg_w_i, eye).reshape(depth, w, w).astype(BF16)
    wb = w_branch.astype(BF16)
    wo = w_out.astype(BF16)
    wup = ffn_w_up.astype(BF16)
    wdn = ffn_w_down.astype(BF16)
    row = lambda a: a[:, None, :]
    g1, g2, rcb, rbr, rbi, lam, fcb = map(row, (norm1_g, norm2_g, rg_conv_b, rg_b_r, rg_b_i, rg_lambda, ffn_conv_b))
    gf = final_norm_g[None, :]

    cache_kt = jnp.transpose(cache_k, (0, 1, 3, 4, 2))
    cache_vt = jnp.transpose(cache_v, (0, 1, 3, 4, 2))
    cache_lft = jnp.transpose(cache_logf, (0, 1, 3, 2))

    tm_p = _tile(n_prompt, 1024)
    tt = _tile(seq, 256)
    tq = _tile(seq, 512)
    tf = _tile(seq, 512)
    xp = x_prompt.reshape(n_prompt, d)
    xs = x_sample.reshape(nd, d)
    sp, ss = [], []
    for l in range(depth):
        last = l == depth - 1
        proj, f, vt = _inproj(xp, g1[l], w_main[l], w_f[l], tm_p, 1024, wvt=w_vt[l])
        proj3 = proj.reshape(batch, seq, proj.shape[1])
        logf, qaug, kaug, stats = _attn_prep(f, b_f[l], proj3, _tile(seq, 1024), tq)
        ya, yc, su, sx, sh = _mixers_prompt(proj3, conv_a_w[l], rg_conv_w[l], rcb[l], wr_bd[l], wi_bd[l],
                                            rbr[l], rbi[l], lam[l], tt)
        yb = _flash(qaug, kaug, vt, stats, tq)
        x1 = _merge(xp, ya.reshape(n_prompt, w), yb.reshape(n_prompt, w), yc.reshape(n_prompt, w), proj, wb[l], wo[l],
                    _tile(n_prompt, 512))
        xp, sf = _ffn(x1, g2[l], wup[l], ffn_conv_w[l], fcb[l], wdn[l], tf, 1536,
                      final_g=gf if last else None, tiles_per_seq=seq // tf)
        sp.append((su[:, SUBLANES - 2:], proj3[:, :, COL_K * w:(COL_K + 1) * w].reshape(batch, seq, N_HEADS, HEAD_DIM),
                   proj3[:, :, COL_V * w:(COL_V + 1) * w].reshape(batch, seq, N_HEADS, HEAD_DIM),
                   logf[:, :N_HEADS].reshape(batch, seq, N_HEADS), sx[:, SUBLANES - 3:], sh[:, 0], sf[:, SUBLANES - 2:]))
        projs, fs = _inproj(xs, g1[l], w_main[l], w_f[l], nd, 1024)
        ha, hc, hf = state_conv_a[l], state_rg_conv[l], state_ffn_conv[l]
        yas, ycs, us, hs, lfs = _mixers_sample(projs, ha[:, 0], ha[:, 1], hc[:, 0], hc[:, 1], hc[:, 2], state_rg_h[l],
                                               conv_a_w[l], rg_conv_w[l], rcb[l], wr_bd[l], wi_bd[l], rbr[l], rbi[l],
                                               lam[l], fs, b_f[l])
        q3 = projs[:, COL_Q * w:(COL_Q + 1) * w].reshape(nd, N_HEADS, HEAD_DIM)
        k3 = projs[:, COL_K * w:(COL_K + 1) * w].reshape(nd, N_HEADS, HEAD_DIM)
        v3 = projs[:, COL_V * w:(COL_V + 1) * w].reshape(nd, N_HEADS, HEAD_DIM)
        ybs = _paged_attn(page_table, q3, jnp.swapaxes(q3, 1, 2), k3, projs[:, None, COL_V * w:(COL_V + 1) * w],
                          jnp.broadcast_to(lfs[:, :N_HEADS, None], (nd, N_HEADS, LANES)),
                          cache_kt, cache_vt, cache_lft, l)
        x1s = _merge(xs, yas, ybs.reshape(nd, w).astype(BF16), ycs, projs, wb[l], wo[l], nd)
        xs, ups = _ffn(x1s, g2[l], wup[l], ffn_conv_w[l], fcb[l], wdn[l], nd, 512, hist=(hf[:, 0], hf[:, 1]),
                       final_g=gf if last else None)
        ss.append((jnp.stack([ha[:, 1], us], axis=1), k3[:, None], v3[:, None], lfs[:, None, :N_HEADS],
                   jnp.stack([hc[:, 1], hc[:, 2], projs[:, COL_XC * w:(COL_XC + 1) * w]], axis=1), hs,
                   jnp.stack([hf[:, 1], ups], axis=1)))

    def stk(outs, i):
        return jnp.stack([o[i] for o in outs], axis=0)

    return (xp.reshape(batch, seq, d), xs.reshape(nd, 1, d),
            stk(sp, 0), stk(sp, 1), stk(sp, 2), stk(sp, 3), stk(sp, 4), stk(sp, 5), stk(sp, 6),
            stk(ss, 0), stk(ss, 1), stk(ss, 2), stk(ss, 3), stk(ss, 4), stk(ss, 5), stk(ss, 6))
```

```python
import functools

import jax
import jax.numpy as jnp
from jax import lax
from jax.experimental import pallas as pl
from jax.experimental.pallas import tpu as pltpu

F32 = jnp.float32
BF16 = jnp.bfloat16

EPS = 1e-6
LRU_C = 8.0
N_HEADS = 8
HEAD_DIM = 64
W_MIX = N_HEADS * HEAD_DIM
NEG = -1e30
LOG2E = 1.4426950408889634
FLASH_LOOKAHEAD = 2
NORM_SLACK = 1.02
SKIP_LOG2 = 158.0
LANES = 128
SUBLANES = 8
VMEM_LIMIT = 56 * 1024 * 1024

COL_BA, COL_CA, COL_XA, COL_Q, COL_K, COL_V, COL_XC, COL_GC = range(8)
COL_GATES = 4


def _params(*sem):
    return pltpu.CompilerParams(dimension_semantics=sem, vmem_limit_bytes=VMEM_LIMIT)


def _rmsnorm(x, g):
    ms = jnp.mean(x * x, axis=-1, keepdims=True)
    return x * lax.rsqrt(ms + EPS) * g


def _softplus(y):
    return jnp.maximum(y, 0.0) + jnp.log1p(jnp.exp(-jnp.abs(y)))


def _log_sigmoid(z):
    return -_softplus(-z)


def _bdot(a, b):
    return jnp.dot(a.astype(BF16), b.astype(BF16), preferred_element_type=F32)


def _split3(x):
    hi = x.astype(BF16)
    r1 = x - hi.astype(F32)
    mid = r1.astype(BF16)
    lo = (r1 - mid.astype(F32)).astype(BF16)
    return hi, mid, lo


def _shift_rows(x, k, tail):
    y = pltpu.roll(x, k, axis=0)
    row = lax.broadcasted_iota(jnp.int32, tail.shape, 0)
    first = jnp.where(row < k, pltpu.roll(tail, k, axis=0), y[:SUBLANES])
    return jnp.concatenate([first, y[SUBLANES:]], axis=0)


def _inproj_kernel(*refs, with_vt):
    if with_vt:
        x_ref, g_ref, w_ref, wf_ref, wvt_ref, proj_ref, f_ref, vt_ref, h_scr = refs
    else:
        x_ref, g_ref, w_ref, wf_ref, proj_ref, f_ref, h_scr = refs

    @pl.when(pl.program_id(1) == 0)
    def _():
        hb = _rmsnorm(x_ref[...], g_ref[...]).astype(BF16)
        h_scr[...] = hb
        f_ref[...] = jnp.dot(hb, wf_ref[...], preferred_element_type=F32)
        if with_vt:
            vt = lax.dot_general(wvt_ref[...], hb, (((1,), (1,)), ((), ())), preferred_element_type=F32)
            vt_ref[...] = vt.astype(BF16)

    proj_ref[...] = jnp.dot(h_scr[...], w_ref[...], preferred_element_type=F32)


def _inproj(x, g, w, wf, tm, tn, wvt=None):
    n, d = x.shape
    nw = w.shape[1]
    in_specs = [
        pl.BlockSpec((tm, d), lambda i, j: (i, 0)),
        pl.BlockSpec((1, d), lambda i, j: (0, 0)),
        pl.BlockSpec((d, tn), lambda i, j: (0, j)),
        pl.BlockSpec((d, LANES), lambda i, j: (0, 0)),
    ]
    out_shape = [jax.ShapeDtypeStruct((n, nw), F32), jax.ShapeDtypeStruct((n, LANES), F32)]
    out_specs = [pl.BlockSpec((tm, tn), lambda i, j: (i, j)), pl.BlockSpec((tm, LANES), lambda i, j: (i, 0))]
    args = [x, g, w, wf]
    if wvt is not None:
        in_specs.append(pl.BlockSpec((W_MIX, d), lambda i, j: (0, 0)))
        out_shape.append(jax.ShapeDtypeStruct((W_MIX, n), BF16))
        out_specs.append(pl.BlockSpec((W_MIX, tm), lambda i, j: (0, i)))
        args.append(wvt)
    return pl.pallas_call(
        functools.partial(_inproj_kernel, with_vt=wvt is not None),
        out_shape=tuple(out_shape),
        grid=(n // tm, nw // tn),
        in_specs=in_specs,
        out_specs=tuple(out_specs),
        scratch_shapes=[pltpu.VMEM((tm, d), BF16)],
        compiler_params=_params("parallel", "arbitrary"),
        name="inproj",
    )(*args)


def _attn_prep_kernel(f_ref, b_ref, q_ref, k_ref, place_ref, ones_ref, e_ref, hsum_ref,
                      logf_ref, qa_ref, ka_ref, st_ref, carry, *, tile):
    @pl.when(pl.program_id(1) == 0)
    def _():
        carry[...] = jnp.zeros_like(carry)

    lf = _log_sigmoid(f_ref[...] + b_ref[...])
    logf_ref[...] = lf
    tc = lf.shape[0]
    row = lax.broadcasted_iota(jnp.int32, lf.shape, 0)
    c = lf
    s = 1
    while s < tc:
        c = c + jnp.where(row >= s, pltpu.roll(c, s, axis=0), 0.0)
        s *= 2
    c = c + carry[...]
    carry[...] = c[tc - 1:tc, :]
    c2 = c * LOG2E
    c_hi, c_mid, c_lo = _split3(c2)
    qb = (q_ref[...] * (LOG2E * HEAD_DIM ** -0.5)).astype(BF16)
    kb = k_ref[...].astype(BF16)
    qa_ref[...] = (jnp.dot(qb, place_ref[...], preferred_element_type=F32) + ones_ref[...]).astype(BF16)
    ka = (jnp.dot(kb, place_ref[...], preferred_element_type=F32)
          + jnp.dot(c_hi, e_ref[0], preferred_element_type=F32)
          + jnp.dot(c_mid, e_ref[1], preferred_element_type=F32)
          + jnp.dot(c_lo, e_ref[2], preferred_element_type=F32))
    ka_ref[...] = ka.astype(BF16)

    def sqnorm(xb):
        xf = xb.astype(F32)
        return jnp.dot((xf * xf).astype(BF16), hsum_ref[...], preferred_element_type=F32)

    kn2, qn2 = sqnorm(kb), sqnorm(qb)
    st_ref[...] = jnp.zeros_like(st_ref)
    for t in range(tc // tile):
        rows = slice(t * tile, (t + 1) * tile)
        st_ref[t, 0:1, :] = jnp.sqrt(jnp.max(kn2[rows, :], axis=0, keepdims=True) * NORM_SLACK)
        st_ref[t, 1:2, :] = jnp.sqrt(jnp.max(qn2[rows, :], axis=0, keepdims=True) * NORM_SLACK)
        st_ref[t, 2:3, :] = jnp.max(-c2[rows, :], axis=0, keepdims=True)


def _attn_prep(f, b_pad, proj3, tc, tile):
    batch, t, _ = proj3.shape
    n = batch * t
    nt = t // tc
    w = W_MIX
    wa = N_HEADS * LANES
    head_sum = (jnp.arange(w, dtype=jnp.int32)[:, None] // HEAD_DIM == jnp.arange(LANES, dtype=jnp.int32)[None, :]).astype(BF16)
    src = jnp.arange(w, dtype=jnp.int32)
    dst = jnp.arange(wa, dtype=jnp.int32)
    place = (dst[None, :] == (src[:, None] // HEAD_DIM) * LANES + src[:, None] % HEAD_DIM).astype(BF16)
    lane = dst % LANES
    ones_row = ((lane >= HEAD_DIM) & (lane < HEAD_DIM + 3)).astype(F32)[None, :]
    hsrc = jnp.arange(LANES, dtype=jnp.int32)
    e = jnp.stack([-((dst[None, :] == hsrc[:, None] * LANES + HEAD_DIM + part) & (hsrc[:, None] < N_HEADS)).astype(F32)
                   for part in range(3)]).astype(BF16)
    per = tc // tile
    return pl.pallas_call(
        functools.partial(_attn_prep_kernel, tile=tile),
        out_shape=(jax.ShapeDtypeStruct((n, LANES), F32), jax.ShapeDtypeStruct((batch, t, wa), BF16),
                   jax.ShapeDtypeStruct((batch, t, wa), BF16),
                   jax.ShapeDtypeStruct((batch, t // tile, SUBLANES, LANES), F32)),
        grid=(batch, nt),
        in_specs=[
            pl.BlockSpec((tc, LANES), lambda b, i: (b * nt + i, 0)),
            pl.BlockSpec((1, LANES), lambda b, i: (0, 0)),
            pl.BlockSpec((None, tc, w), lambda b, i: (b, i, COL_Q)),
            pl.BlockSpec((None, tc, w), lambda b, i: (b, i, COL_K)),
            pl.BlockSpec((w, wa), lambda b, i: (0, 0)),
            pl.BlockSpec((1, wa), lambda b, i: (0, 0)),
            pl.BlockSpec((3, LANES, wa), lambda b, i: (0, 0, 0)),
            pl.BlockSpec((w, LANES), lambda b, i: (0, 0)),
        ],
        out_specs=(
            pl.BlockSpec((tc, LANES), lambda b, i: (b * nt + i, 0)),
            pl.BlockSpec((None, tc, wa), lambda b, i: (b, i, 0)),
            pl.BlockSpec((None, tc, wa), lambda b, i: (b, i, 0)),
            pl.BlockSpec((None, per, SUBLANES, LANES), lambda b, i: (b, i, 0, 0)),
        ),
        scratch_shapes=[pltpu.VMEM((1, LANES), F32)],
        compiler_params=_params("parallel", "arbitrary"),
        name="attn_prep",
    )(f, b_pad, proj3, proj3, place, ones_row, e, head_sum)


def _rglru_terms(xc, wr_ref, wi_ref, br_ref, bi_ref, lam_ref):
    r = jax.nn.sigmoid(_bdot(xc, wr_ref[...]) + br_ref[...])
    i = jax.nn.sigmoid(_bdot(xc, wi_ref[...]) + bi_ref[...])
    log_a = -LRU_C * r * _softplus(-lam_ref[...])
    a = jnp.exp(log_a)
    bterm = jnp.sqrt(-jnp.tanh(log_a) * (1.0 + a * a)) * i * xc
    return a, bterm


def _mixers_prompt_kernel(ba_ref, ca_ref, xa_ref, xc_ref, gc_ref, caw_ref, rcw_ref, rcb_ref,
                          wr_ref, wi_ref, br_ref, bi_ref, lam_ref,
                          ya_ref, yc_ref, su_ref, sx_ref, sh_ref, u_tail, x_tail, h_carry):
    @pl.when(pl.program_id(1) == 0)
    def _():
        u_tail[...] = jnp.zeros_like(u_tail)
        x_tail[...] = jnp.zeros_like(x_tail)
        h_carry[...] = jnp.zeros_like(h_carry)

    tt = ba_ref.shape[0]
    u = ca_ref[...] * xa_ref[...]
    ut = u_tail[...]
    ua = caw_ref[0:1, :] * _shift_rows(u, 2, ut) + caw_ref[1:2, :] * _shift_rows(u, 1, ut) + caw_ref[2:3, :] * u
    ya_ref[...] = (ba_ref[...] * ua).astype(BF16)
    u_tail[...] = u[tt - SUBLANES:, :]
    su_ref[...] = u[tt - SUBLANES:, :]
    x = xc_ref[...]
    xt = x_tail[...]
    xc = (rcw_ref[0:1, :] * _shift_rows(x, 3, xt) + rcw_ref[1:2, :] * _shift_rows(x, 2, xt)
          + rcw_ref[2:3, :] * _shift_rows(x, 1, xt) + rcw_ref[3:4, :] * x + rcb_ref[...])
    x_tail[...] = x[tt - SUBLANES:, :]
    sx_ref[...] = x[tt - SUBLANES:, :]
    a, b = _rglru_terms(xc, wr_ref, wi_ref, br_ref, bi_ref, lam_ref)
    row = lax.broadcasted_iota(jnp.int32, a.shape, 0)
    s = 1
    while s < tt:
        keep = row >= s
        a_prev = jnp.where(keep, pltpu.roll(a, s, axis=0), 1.0)
        b_prev = jnp.where(keep, pltpu.roll(b, s, axis=0), 0.0)
        b = a * b_prev + b
        a = a * a_prev
        s *= 2
    h = a * h_carry[...] + b
    h_carry[...] = h[tt - 1:tt, :]
    sh_ref[...] = h[tt - 1:tt, :]
    yc_ref[...] = (h * jax.nn.gelu(gc_ref[...])).astype(BF16)


def _mixers_prompt(proj3, caw, rcw, rcb, wr, wi, br, bi, lam, tt):
    batch, t, _ = proj3.shape
    w = W_MIX
    nt = t // tt

    def col(c):
        return pl.BlockSpec((None, tt, w), lambda b, i, c=c: (b, i, c))

    def const(shape):
        return pl.BlockSpec(shape, lambda b, i: (0,) * len(shape))

    return pl.pallas_call(
        _mixers_prompt_kernel,
        out_shape=(
            jax.ShapeDtypeStruct((batch, t, w), BF16),
            jax.ShapeDtypeStruct((batch, t, w), BF16),
            jax.ShapeDtypeStruct((batch, SUBLANES, w), F32),
            jax.ShapeDtypeStruct((batch, SUBLANES, w), F32),
            jax.ShapeDtypeStruct((batch, 1, w), F32),
        ),
        grid=(batch, nt),
        in_specs=[col(COL_BA), col(COL_CA), col(COL_XA), col(COL_XC), col(COL_GC),
                  const((3, w)), const((4, w)), const((1, w)), const((w, w)), const((w, w)),
                  const((1, w)), const((1, w)), const((1, w))],
        out_specs=(
            pl.BlockSpec((None, tt, w), lambda b, i: (b, i, 0)),
            pl.BlockSpec((None, tt, w), lambda b, i: (b, i, 0)),
            pl.BlockSpec((None, SUBLANES, w), lambda b, i: (b, 0, 0)),
            pl.BlockSpec((None, SUBLANES, w), lambda b, i: (b, 0, 0)),
            pl.BlockSpec((None, 1, w), lambda b, i: (b, 0, 0)),
        ),
        scratch_shapes=[pltpu.VMEM((SUBLANES, w), F32), pltpu.VMEM((SUBLANES, w), F32), pltpu.VMEM((1, w), F32)],
        compiler_params=_params("parallel", "arbitrary"),
        name="mixers_prompt",
    )(proj3, proj3, proj3, proj3, proj3, caw, rcw, rcb, wr, wi, br, bi, lam)


def _mixers_sample_kernel(ba_ref, ca_ref, xa_ref, xc_ref, gc_ref, ha0_ref, ha1_ref, hc0_ref, hc1_ref, hc2_ref, h0_ref,
                          caw_ref, rcw_ref, rcb_ref, wr_ref, wi_ref, br_ref, bi_ref, lam_ref, f_ref, bf_ref,
                          ya_ref, yc_ref, u_ref, h_ref, lf_ref):
    lf_ref[...] = _log_sigmoid(f_ref[...] + bf_ref[...])
    u = ca_ref[...] * xa_ref[...]
    ua = caw_ref[0:1, :] * ha0_ref[...] + caw_ref[1:2, :] * ha1_ref[...] + caw_ref[2:3, :] * u
    ya_ref[...] = (ba_ref[...] * ua).astype(BF16)
    u_ref[...] = u
    xc = (rcw_ref[0:1, :] * hc0_ref[...] + rcw_ref[1:2, :] * hc1_ref[...] + rcw_ref[2:3, :] * hc2_ref[...]
          + rcw_ref[3:4, :] * xc_ref[...] + rcb_ref[...])
    a, b = _rglru_terms(xc, wr_ref, wi_ref, br_ref, bi_ref, lam_ref)
    h = a * h0_ref[...] + b
    h_ref[...] = h
    yc_ref[...] = (h * jax.nn.gelu(gc_ref[...])).astype(BF16)


def _mixers_sample(proj, ha0, ha1, hc0, hc1, hc2, h0, caw, rcw, rcb, wr, wi, br, bi, lam, f, bf):
    n = proj.shape[0]
    w = W_MIX

    def col(c):
        return pl.BlockSpec((n, w), lambda i, c=c: (0, c))

    def full(shape):
        return pl.BlockSpec(shape, lambda i: (0,) * len(shape))

    return pl.pallas_call(
        _mixers_sample_kernel,
        out_shape=(jax.ShapeDtypeStruct((n, w), BF16), jax.ShapeDtypeStruct((n, w), BF16),
                   jax.ShapeDtypeStruct((n, w), F32), jax.ShapeDtypeStruct((n, w), F32),
                   jax.ShapeDtypeStruct((n, LANES), F32)),
        grid=(1,),
        in_specs=[col(COL_BA), col(COL_CA), col(COL_XA), col(COL_XC), col(COL_GC)]
        + [full((n, w))] * 6
        + [full((3, w)), full((4, w)), full((1, w)), full((w, w)), full((w, w)), full((1, w)), full((1, w)), full((1, w)),
           full((n, LANES)), full((1, LANES))],
        out_specs=(full((n, w)), full((n, w)), full((n, w)), full((n, w)), full((n, LANES))),
        compiler_params=_params("arbitrary"),
        name="mixers_sample",
    )(proj, proj, proj, proj, proj, ha0, ha1, hc0, hc1, hc2, h0, caw, rcw, rcb, wr, wi, br, bi, lam, f, bf)


def _flash_kernel(qi_ref, ki_ref, qn_ref, kn_ref, cn_ref, qa_ref, ka_ref, vt_ref, o_ref, m_s, l_s, acc, *, nq):
    b = pl.program_id(0)
    p = pl.program_id(1)
    qi = qi_ref[p]
    ki = ki_ref[p]
    tq = qa_ref.shape[0]
    tk = ka_ref.shape[0]

    @pl.when(ki == qi)
    def _():
        m_s[...] = jnp.full_like(m_s, NEG)
        l_s[...] = jnp.zeros_like(l_s)
        acc[...] = jnp.zeros_like(acc)

    def step(diagonal):
        if diagonal:
            krow = lax.broadcasted_iota(jnp.int32, (tk, tq), 0)
            qcol = lax.broadcasted_iota(jnp.int32, (tk, tq), 1)
            causal = krow <= qcol

        def scores(h):
            la = slice(h * LANES, (h + 1) * LANES)
            return lax.dot_general(ka_ref[:, la], qa_ref[:, la], (((1,), (1,)), ((), ())),
                                   preferred_element_type=F32)

        sts = [scores(h) for h in range(FLASH_LOOKAHEAD)]
        for h in range(N_HEADS):
            rows = slice(h * HEAD_DIM, (h + 1) * HEAD_DIM)
            if h + FLASH_LOOKAHEAD < N_HEADS:
                sts.append(scores(h + FLASH_LOOKAHEAD))
            st = sts[h]
            if diagonal:
                st = jnp.where(causal, st, NEG)
            m_prev = m_s[h:h + 1, :]
            m_new = jnp.maximum(m_prev, jnp.max(st, axis=0, keepdims=True))
            alpha = jnp.exp2(m_prev - m_new)
            pt = jnp.exp2(st - m_new)
            l_s[h:h + 1, :] = alpha * l_s[h:h + 1, :] + jnp.sum(pt, axis=0, keepdims=True)
            acc[rows, :] = alpha * acc[rows, :] + jnp.dot(vt_ref[rows, :], pt.astype(BF16),
                                                          preferred_element_type=F32)
            m_s[h:h + 1, :] = m_new

    @pl.when(ki == qi)
    def _():
        step(True)

    @pl.when(ki < qi)
    def _():
        m_min = jnp.min(m_s[...], axis=1, keepdims=True)
        gap = None
        for h in range(N_HEADS):
            iq = (b * nq + qi) * N_HEADS + h
            ik = (b * nq + ki) * N_HEADS + h
            g = qn_ref[iq] * kn_ref[ik] + cn_ref[ik] - m_min[h, 0]
            gap = g if gap is None else jnp.maximum(gap, g)

        @pl.when(gap > -SKIP_LOG2)
        def _():
            step(False)

    @pl.when(ki == 0)
    def _():
        for h in range(N_HEADS):
            rows = slice(h * HEAD_DIM, (h + 1) * HEAD_DIM)
            acc[rows, :] = acc[rows, :] / l_s[h:h + 1, :]
        o_ref[...] = acc[...].T.astype(BF16)


def _flash(qaug, kaug, vt, stats, tq):
    batch, t, wa = qaug.shape
    w = W_MIX
    nq = t // tq
    pairs = [(i, j) for i in range(nq) for j in range(i, -1, -1)]
    qi_tab = jnp.asarray([i for i, _ in pairs], dtype=jnp.int32)
    ki_tab = jnp.asarray([j for _, j in pairs], dtype=jnp.int32)
    kn, qn, cn = (stats[:, :, r, :N_HEADS].reshape(-1) for r in range(3))
    grid_spec = pltpu.PrefetchScalarGridSpec(
        num_scalar_prefetch=5,
        grid=(batch, len(pairs)),
        in_specs=[
            pl.BlockSpec((None, tq, wa), lambda b, p, qi, ki, *_: (b, qi[p], 0)),
            pl.BlockSpec((None, tq, wa), lambda b, p, qi, ki, *_: (b, ki[p], 0)),
            pl.BlockSpec((w, tq), lambda b, p, qi, ki, *_: (0, b * nq + ki[p])),
        ],
        out_specs=pl.BlockSpec((None, tq, w), lambda b, p, qi, ki, *_: (b, qi[p], 0)),
        scratch_shapes=[pltpu.VMEM((N_HEADS, tq), F32), pltpu.VMEM((N_HEADS, tq), F32), pltpu.VMEM((w, tq), F32)],
    )
    return pl.pallas_call(
        functools.partial(_flash_kernel, nq=nq),
        out_shape=jax.ShapeDtypeStruct((batch, t, w), BF16),
        grid_spec=grid_spec,
        compiler_params=_params("parallel", "arbitrary"),
        name="flash_prompt",
    )(qi_tab, ki_tab, qn, kn, cn, qaug, kaug, vt)


def _paged_attn_kernel(pt_ref, q_ref, qt_ref, kn_ref, vrow_ref, lfn_ref, tri_ref, *refs, n_pages):
    k_refs = refs[:n_pages]
    v_refs = refs[n_pages:2 * n_pages]
    lf_refs = refs[2 * n_pages:3 * n_pages]
    o_ref = refs[3 * n_pages]
    acc = refs[3 * n_pages + 1]
    page = lf_refs[0].shape[-1]
    scale = HEAD_DIM ** -0.5

    lf = jnp.concatenate([r[...] for r in lf_refs], axis=0)
    tri = tri_ref[...]
    hi, mid, lo = _split3(lf)
    w = (jnp.dot(hi, tri, preferred_element_type=F32) + jnp.dot(mid, tri, preferred_element_type=F32)
         + jnp.dot(lo, tri, preferred_element_type=F32))
    tot = jnp.broadcast_to(w[:, page - 1:page], w.shape)
    row = lax.broadcasted_iota(jnp.int32, w.shape, 0)
    off = tot
    s = N_HEADS
    while s < n_pages * N_HEADS:
        off = off + jnp.where(row >= s, pltpu.roll(off, s, axis=0), 0.0)
        s *= 2
    c = w + (off - tot)

    s_new = jnp.sum(q_ref[...] * kn_ref[...], axis=-1, keepdims=True) * scale
    lane = lax.broadcasted_iota(jnp.int32, (1, W_MIX), 1)
    l_row = jnp.ones((1, W_MIX), F32)
    pn_row = jnp.zeros((1, W_MIX), F32)
    for h in range(N_HEADS):
        qcol = jnp.broadcast_to(qt_ref[:, h:h + 1] * scale, (HEAD_DIM, page))
        zs = []
        for j in range(n_pages):
            sj = jnp.sum(k_refs[j][h] * qcol, axis=0, keepdims=True)
            zs.append(sj - c[j * N_HEADS + h:j * N_HEADS + h + 1, :])
        last = (n_pages - 1) * N_HEADS + h
        zn = s_new[h:h + 1, :] - (off[last:last + 1, 0:1] + lfn_ref[h:h + 1, 0:1])
        mz = zs[0]
        for z in zs[1:]:
            mz = jnp.maximum(mz, z)
        m = jnp.maximum(jnp.max(mz, axis=-1, keepdims=True), zn)
        pn = jnp.exp(zn - m)
        lsum = jnp.zeros((1, page), F32)
        a = jnp.zeros((HEAD_DIM, page), F32)
        for j in range(n_pages):
            p = jnp.exp(zs[j] - m)
            lsum = lsum + p
            a = a + v_refs[j][h] * p
        acc[h * HEAD_DIM:(h + 1) * HEAD_DIM, :] = a
        l = jnp.sum(lsum, axis=-1, keepdims=True) + pn
        mine = (lane >= h * HEAD_DIM) & (lane < (h + 1) * HEAD_DIM)
        l_row = jnp.where(mine, l, l_row)
        pn_row = jnp.where(mine, pn, pn_row)
    ones = jnp.ones((SUBLANES, page), BF16)
    nt = (((1,), (1,)), ((), ()))
    ahi, amid, alo = _split3(acc[...])
    o_pages = (lax.dot_general(ones, ahi, nt, preferred_element_type=F32)
               + lax.dot_general(ones, amid, nt, preferred_element_type=F32)
               + lax.dot_general(ones, alo, nt, preferred_element_type=F32))
    o_ref[...] = (o_pages[0:1, :] + pn_row * vrow_ref[...]) / l_row


def _paged_attn(page_table, q3, qt3, kn3, vrow, lfn_rep, cache_kt, cache_vt, cache_lft, layer):
    n = q3.shape[0]
    n_pages = page_table.shape[1]
    page = cache_kt.shape[-1]
    tri = (jnp.arange(page)[:, None] <= jnp.arange(page)[None, :]).astype(BF16)

    def per_seq(shape):
        return pl.BlockSpec((None,) + shape, lambda b, pt: (b, 0, 0))

    def kv(j):
        return pl.BlockSpec((None, None, N_HEADS, HEAD_DIM, page), lambda b, pt, j=j: (layer, pt[b * n_pages + j], 0, 0, 0))

    def lfp(j):
        return pl.BlockSpec((None, None, N_HEADS, page), lambda b, pt, j=j: (layer, pt[b * n_pages + j], 0, 0))

    grid_spec = pltpu.PrefetchScalarGridSpec(
        num_scalar_prefetch=1,
        grid=(n,),
        in_specs=[per_seq((N_HEADS, HEAD_DIM)), per_seq((HEAD_DIM, N_HEADS)), per_seq((N_HEADS, HEAD_DIM)),
                  per_seq((1, W_MIX)), per_seq((N_HEADS, LANES)), pl.BlockSpec((page, page), lambda b, pt: (0, 0))]
        + [kv(j) for j in range(n_pages)] + [kv(j) for j in range(n_pages)] + [lfp(j) for j in range(n_pages)],
        out_specs=per_seq((1, W_MIX)),
        scratch_shapes=[pltpu.VMEM((W_MIX, page), F32)],
    )
    return pl.pallas_call(
        functools.partial(_paged_attn_kernel, n_pages=n_pages),
        out_shape=jax.ShapeDtypeStruct((n, 1, W_MIX), F32),
        grid_spec=grid_spec,
        compiler_params=_params("arbitrary"),
        name="paged_attn",
    )(page_table.reshape(-1), q3, qt3, kn3, vrow, lfn_rep, tri,
      *([cache_kt] * n_pages), *([cache_vt] * n_pages), *([cache_lft] * n_pages))


def _merge_kernel(x_ref, ya_ref, yb_ref, yc_ref, g0_ref, g1_ref, g2_ref, wb_ref, wo_ref, o_ref):
    w = W_MIX
    merged = (jax.nn.sigmoid(g0_ref[...]) * jnp.dot(ya_ref[...], wb_ref[0:w, :], preferred_element_type=F32)
              + jax.nn.sigmoid(g1_ref[...]) * jnp.dot(yb_ref[...], wb_ref[w:2 * w, :], preferred_element_type=F32)
              + jax.nn.sigmoid(g2_ref[...]) * jnp.dot(yc_ref[...], wb_ref[2 * w:3 * w, :], preferred_element_type=F32))
    o_ref[...] = x_ref[...] + jnp.dot(merged.astype(BF16), wo_ref[...], preferred_element_type=F32)


def _merge(x, ya, yb, yc, proj, wb, wo, tm):
    n, d = x.shape
    w = W_MIX

    def rows(width, c=0):
        return pl.BlockSpec((tm, width), lambda i, c=c: (i, c))

    return pl.pallas_call(
        _merge_kernel,
        out_shape=jax.ShapeDtypeStruct((n, d), F32),
        grid=(n // tm,),
        in_specs=[rows(d), rows(w), rows(w), rows(w), rows(d, COL_GATES), rows(d, COL_GATES + 1), rows(d, COL_GATES + 2),
                  pl.BlockSpec((3 * w, d), lambda i: (0, 0)), pl.BlockSpec((d, d), lambda i: (0, 0))],
        out_specs=rows(d),
        compiler_params=_params("parallel"),
        name="merge",
    )(x, ya, yb, yc, proj, proj, proj, wb, wo)


def _ffn_kernel(*refs, chunk, hist, final, tiles_per_seq):
    if hist:
        x_ref, g_ref, wup_ref, cw_ref, cb_ref, wdn_ref, h0_ref, h1_ref = refs[:8]
        refs = refs[8:]
    else:
        x_ref, g_ref, wup_ref, cw_ref, cb_ref, wdn_ref = refs[:6]
        refs = refs[6:]
    if final:
        gf_ref = refs[0]
        refs = refs[1:]
    if hist:
        o_ref, st_ref = refs
    else:
        o_ref, st_ref, tail = refs

        @pl.when(pl.program_id(0) % tiles_per_seq == 0)
        def _():
            tail[...] = jnp.zeros_like(tail)

    x = x_ref[...]
    tm = x.shape[0]
    dff = wdn_ref.shape[0]
    hb = _rmsnorm(x, g_ref[...]).astype(BF16)
    acc = jnp.zeros(x.shape, F32)
    for c in range(dff // chunk):
        halves = []
        for off in (c * chunk, dff + c * chunk):
            sl = slice(off, off + chunk)
            up = jnp.dot(hb, wup_ref[:, sl], preferred_element_type=F32)
            if hist:
                prev2, prev1 = h0_ref[:, sl], h1_ref[:, sl]
                st_ref[:, sl] = up
            else:
                tl = tail[:, sl]
                prev2, prev1 = _shift_rows(up, 2, tl), _shift_rows(up, 1, tl)
                tail[:, sl] = up[tm - SUBLANES:, :]
                st_ref[:, sl] = up[tm - SUBLANES:, :]
            halves.append(cw_ref[0:1, sl] * prev2 + cw_ref[1:2, sl] * prev1 + cw_ref[2:3, sl] * up + cb_ref[:, sl])
        act = jax.nn.gelu(halves[0]) * halves[1]
        acc = acc + jnp.dot(act.astype(BF16), wdn_ref[c * chunk:(c + 1) * chunk, :], preferred_element_type=F32)
    y = x + acc
    if final:
        y = _rmsnorm(y, gf_ref[...])
    o_ref[...] = y


def _ffn(x, g, wup, cw, cb, wdn, tm, chunk, hist=None, final_g=None, tiles_per_seq=1):
    n, d = x.shape
    dup = wup.shape[1]
    nt = n // tm
    resident = dict(pipeline_mode=pl.Buffered(1))
    in_specs = [
        pl.BlockSpec((tm, d), lambda i: (i, 0)),
        pl.BlockSpec((1, d), lambda i: (0, 0)),
        pl.BlockSpec((d, dup), lambda i: (0, 0), **resident),
        pl.BlockSpec((3, dup), lambda i: (0, 0)),
        pl.BlockSpec((1, dup), lambda i: (0, 0)),
        pl.BlockSpec((dup // 2, d), lambda i: (0, 0), **resident),
    ]
    args = [x, g, wup, cw, cb, wdn]
    scratch = []
    if hist is not None:
        in_specs += [pl.BlockSpec((tm, dup), lambda i: (i, 0))] * 2
        args += list(hist)
        st_shape = jax.ShapeDtypeStruct((n, dup), F32)
        st_spec = pl.BlockSpec((tm, dup), lambda i: (i, 0))
    else:
        nseq = nt // tiles_per_seq
        st_shape = jax.ShapeDtypeStruct((nseq, SUBLANES, dup), F32)
        st_spec = pl.BlockSpec((None, SUBLANES, dup), lambda i: (i // tiles_per_seq, 0, 0))
        scratch = [pltpu.VMEM((SUBLANES, dup), F32)]
    if final_g is not None:
        in_specs.append(pl.BlockSpec((1, d), lambda i: (0, 0)))
        args.append(final_g)
    return pl.pallas_call(
        functools.partial(_ffn_kernel, chunk=chunk, hist=hist is not None, final=final_g is not None,
                          tiles_per_seq=tiles_per_seq),
        out_shape=(jax.ShapeDtypeStruct((n, d), F32), st_shape),
        grid=(nt,),
        in_specs=in_specs,
        out_specs=(pl.BlockSpec((tm, d), lambda i: (i, 0)), st_spec),
        scratch_shapes=scratch,
        compiler_params=_params("arbitrary"),
        name="ffn",
    )(*args)


def _tile(n, want):
    if n <= want:
        return n
    t = want
    while n % t:
        t -= SUBLANES
    return t


def kernel(x_prompt, x_sample, cache_k, cache_v, cache_logf, page_table, state_conv_a, state_rg_conv, state_rg_h, state_ffn_conv, norm1_g, w_in, conv_a_w, b_forget, rg_conv_w, rg_conv_b, rg_w_r, rg_b_r, rg_w_i, rg_b_i, rg_lambda, w_branch, w_out, norm2_g, ffn_w_up, ffn_conv_w, ffn_conv_b, ffn_w_down, final_norm_g):
    batch, seq, d = x_prompt.shape
    nd = x_sample.shape[0]
    depth = w_in.shape[0]
    w = W_MIX
    n_phys, page = cache_k.shape[1], cache_k.shape[2]
    n_prompt = batch * seq
    f_lo = 6 * w

    w_main = jnp.concatenate([w_in[:, :, :f_lo], w_in[:, :, f_lo + N_HEADS:]], axis=-1).astype(BF16)
    w_f = jnp.pad(w_in[:, :, f_lo:f_lo + N_HEADS], ((0, 0), (0, 0), (0, LANES - N_HEADS))).astype(BF16)
    w_vt = jnp.swapaxes(w_in[:, :, COL_V * w:(COL_V + 1) * w], 1, 2).astype(BF16)
    b_f = jnp.pad(b_forget, ((0, 0), (0, LANES - N_HEADS)))[:, None, :]
    eye = jnp.eye(rg_w_r.shape[1], dtype=F32)
    wr_bd = jnp.einsum("lnde,nm->lndme", rg_w_r, eye).reshape(depth, w, w).astype(BF16)
    wi_bd = jnp.einsum("lnde,nm->lndme", rg_w_i, eye).reshape(depth, w, w).astype(BF16)
    wb = w_branch.astype(BF16)
    wo = w_out.astype(BF16)
    wup = ffn_w_up.astype(BF16)
    wdn = ffn_w_down.astype(BF16)
    row = lambda a: a[:, None, :]
    g1, g2, rcb, rbr, rbi, lam, fcb = map(row, (norm1_g, norm2_g, rg_conv_b, rg_b_r, rg_b_i, rg_lambda, ffn_conv_b))
    gf = final_norm_g[None, :]

    cache_kt = jnp.transpose(cache_k, (0, 1, 3, 4, 2))
    cache_vt = jnp.transpose(cache_v, (0, 1, 3, 4, 2))
    cache_lft = jnp.transpose(cache_logf, (0, 1, 3, 2))

    tm_p = _tile(n_prompt, 1024)
    tt = _tile(seq, 256)
    tq = _tile(seq, 512)
    tf = _tile(seq, 512)
    xp = x_prompt.reshape(n_prompt, d)
    xs = x_sample.reshape(nd, d)
    sp, ss = [], []
    for l in range(depth):
        last = l == depth - 1
        proj, f, vt = _inproj(xp, g1[l], w_main[l], w_f[l], tm_p, 1792, wvt=w_vt[l])
        proj3 = proj.reshape(batch, seq, proj.shape[1])
        logf, qaug, kaug, stats = _attn_prep(f, b_f[l], proj3, _tile(seq, 1024), tq)
        ya, yc, su, sx, sh = _mixers_prompt(proj3, conv_a_w[l], rg_conv_w[l], rcb[l], wr_bd[l], wi_bd[l],
                                            rbr[l], rbi[l], lam[l], tt)
        yb = _flash(qaug, kaug, vt, stats, tq)
        x1 = _merge(xp, ya.reshape(n_prompt, w), yb.reshape(n_prompt, w), yc.reshape(n_prompt, w), proj, wb[l], wo[l],
                    _tile(n_prompt, 512))
        xp, sf = _ffn(x1, g2[l], wup[l], ffn_conv_w[l], fcb[l], wdn[l], tf, 1536,
                      final_g=gf if last else None, tiles_per_seq=seq // tf)
        sp.append((su[:, SUBLANES - 2:], proj3[:, :, COL_K * w:(COL_K + 1) * w].reshape(batch, seq, N_HEADS, HEAD_DIM),
                   proj3[:, :, COL_V * w:(COL_V + 1) * w].reshape(batch, seq, N_HEADS, HEAD_DIM),
                   logf[:, :N_HEADS].reshape(batch, seq, N_HEADS), sx[:, SUBLANES - 3:], sh[:, 0], sf[:, SUBLANES - 2:]))
        projs, fs = _inproj(xs, g1[l], w_main[l], w_f[l], nd, 1024)
        ha, hc, hf = state_conv_a[l], state_rg_conv[l], state_ffn_conv[l]
        yas, ycs, us, hs, lfs = _mixers_sample(projs, ha[:, 0], ha[:, 1], hc[:, 0], hc[:, 1], hc[:, 2], state_rg_h[l],
                                               conv_a_w[l], rg_conv_w[l], rcb[l], wr_bd[l], wi_bd[l], rbr[l], rbi[l],
                                               lam[l], fs, b_f[l])
        q3 = projs[:, COL_Q * w:(COL_Q + 1) * w].reshape(nd, N_HEADS, HEAD_DIM)
        k3 = projs[:, COL_K * w:(COL_K + 1) * w].reshape(nd, N_HEADS, HEAD_DIM)
        v3 = projs[:, COL_V * w:(COL_V + 1) * w].reshape(nd, N_HEADS, HEAD_DIM)
        ybs = _paged_attn(page_table, q3, jnp.swapaxes(q3, 1, 2), k3, projs[:, None, COL_V * w:(COL_V + 1) * w],
                          jnp.broadcast_to(lfs[:, :N_HEADS, None], (nd, N_HEADS, LANES)),
                          cache_kt, cache_vt, cache_lft, l)
        x1s = _merge(xs, yas, ybs.reshape(nd, w).astype(BF16), ycs, projs, wb[l], wo[l], nd)
        xs, ups = _ffn(x1s, g2[l], wup[l], ffn_conv_w[l], fcb[l], wdn[l], nd, 512, hist=(hf[:, 0], hf[:, 1]),
                       final_g=gf if last else None)
        ss.append((jnp.stack([ha[:, 1], us], axis=1), k3[:, None], v3[:, None], lfs[:, None, :N_HEADS],
                   jnp.stack([hc[:, 1], hc[:, 2], projs[:, COL_XC * w:(COL_XC + 1) * w]], axis=1), hs,
                   jnp.stack([hf[:, 1], ups], axis=1)))

    def stk(outs, i):
        return jnp.stack([o[i] for o in outs], axis=0)

    return (xp.reshape(batch, seq, d), xs.reshape(nd, 1, d),
            stk(sp, 0), stk(sp, 1), stk(sp, 2), stk(sp, 3), stk(sp, 4), stk(sp, 5), stk(sp, 6),
            stk(ss, 0), stk(ss, 1), stk(ss, 2), stk(ss, 3), stk(ss, 4), stk(ss, 5), stk(ss, 6))
```
